```python
import math
import jax
import jax.numpy as jnp
from jax import lax
import numpy as np

D_MODEL = 1024
BATCH = 4
SEQ = 4096
DEPTH = 2

GRID_W = 64
CTX_LEN = 256
EPS = 1e-6
ROPE_BASE = 10000.0
Q_BLOCK = 128

MLA_HEADS = 8
MLA_Q_RANK = 256
MLA_KV_RANK = 256
MLA_NOPE = 64
MLA_ROPE = 32
MLA_V = 64
MLA_SCALE = (MLA_NOPE + MLA_ROPE) ** -0.5

DIFF_HEADS = 4
DIFF_DH = 64

SWA_HEADS = 8
SWA_KV_HEADS = 2
SWA_DH = 64
WINDOW = 128
SWA_BLOCK = WINDOW

D_FF = -(-8 * D_MODEL // (3 * 256)) * 256

N_BRANCH = 3
BRANCH_A = MLA_HEADS * MLA_V
BRANCH_B = DIFF_HEADS * 2 * DIFF_DH
BRANCH_C = SWA_HEADS * SWA_DH
KV_SPLITS = (MLA_KV_RANK, MLA_ROPE, 2 * DIFF_HEADS * DIFF_DH, DIFF_HEADS * 2 * DIFF_DH,
             SWA_KV_HEADS * SWA_DH, SWA_KV_HEADS * SWA_DH)
Q_SPLITS = (MLA_Q_RANK, 2 * DIFF_HEADS * DIFF_DH, SWA_HEADS * SWA_DH, N_BRANCH * D_MODEL)
KV_WIDTH = sum(KV_SPLITS)
IN_WIDTH = KV_WIDTH + sum(Q_SPLITS)

kernel_name = 'hybrid_mla_diff_swa_prefix_dit'


def rms_norm(x, w):
    xf = x.astype(jnp.float32)
    y = xf * lax.rsqrt(jnp.mean(xf * xf, axis=-1, keepdims=True) + EPS)
    return (y * w.astype(jnp.float32)).astype(x.dtype)


def split_cols(z, widths):
    return jnp.split(z, np.cumsum(widths)[:-1].tolist(), axis=-1)


def axial_rope_tables(rows, rot_dim):
    r = jnp.repeat(jnp.arange(rows, dtype=jnp.float32), GRID_W)
    col = jnp.tile(jnp.arange(GRID_W, dtype=jnp.float32), rows)
    axis_dim = rot_dim // 2
    inv = ROPE_BASE ** (-jnp.arange(0, axis_dim, 2, dtype=jnp.float32) / axis_dim)
    ang_r = r[:, None] * inv
    ang_c = col[:, None] * inv
    return (jnp.cos(ang_r), jnp.sin(ang_r), jnp.cos(ang_c), jnp.sin(ang_c))


def rope_1d(x, cos, sin):
    x1, x2 = jnp.split(x, 2, axis=-1)
    c = cos[None, :, None, :]
    s = sin[None, :, None, :]
    return jnp.concatenate([x1 * c - x2 * s, x2 * c + x1 * s], axis=-1)


def rope_2d(x, tab):
    cr, sr, cc, sc = tab
    d = x.shape[-1]
    out = jnp.concatenate([rope_1d(x[..., : d // 2], cr, sr), rope_1d(x[..., d // 2:], cc, sc)], axis=-1)
    return out.astype(x.dtype)


def adaln(cond, w, b):
    return jnp.split(jax.nn.silu(cond) @ w + b, 6, axis=-1)


def modulate(h, shift, scale):
    return h * (1.0 + scale) + shift


def project_kv(h, w_kv, kv_norm, w_ukv, tabs):
    B, L, _ = h.shape
    ckv, kr, dk, dv, sk, sv = split_cols(h @ w_kv, KV_SPLITS)
    kv = (rms_norm(ckv, kv_norm) @ w_ukv).reshape(B, L, MLA_HEADS, MLA_NOPE + MLA_V)
    kr = kr.reshape(B, L, 1, MLA_ROPE)
    dk = dk.reshape(B, L, 2 * DIFF_HEADS, DIFF_DH)
    sk = sk.reshape(B, L, SWA_KV_HEADS, SWA_DH)
    if tabs is not None:
        kr = rope_2d(kr, tabs[MLA_ROPE])
        dk = rope_2d(dk, tabs[DIFF_DH])
        sk = rope_2d(sk, tabs[SWA_DH])
    return {
        'mla_k': kv[..., :MLA_NOPE], 'mla_v': kv[..., MLA_NOPE:], 'mla_kr': kr[:, :, 0],
        'diff_k': dk, 'diff_v': dv.reshape(B, L, DIFF_HEADS, 2 * DIFF_DH),
        'swa_k': sk, 'swa_v': sv.reshape(B, L, SWA_KV_HEADS, SWA_DH),
    }


def project_q(h, w_q, q_norm, w_uq, tabs):
    B, L, _ = h.shape
    cq, dq, sq, gates = split_cols(h @ w_q, Q_SPLITS)
    q = (rms_norm(cq, q_norm) @ w_uq).reshape(B, L, MLA_HEADS, MLA_NOPE + MLA_ROPE)
    qn, qr = q[..., :MLA_NOPE], q[..., MLA_NOPE:]
    dq = dq.reshape(B, L, 2 * DIFF_HEADS, DIFF_DH)
    sq = sq.reshape(B, L, SWA_HEADS, SWA_DH)
    if tabs is not None:
        qr = rope_2d(qr, tabs[MLA_ROPE])
        dq = rope_2d(dq, tabs[DIFF_DH])
        sq = rope_2d(sq, tabs[SWA_DH])
    return {'mla_qn': qn, 'mla_qr': qr, 'diff_q': dq, 'swa_q': sq, 'gates': jax.nn.sigmoid(gates)}


def sweep_query_blocks(fn, *qs):
    B, S = qs[0].shape[:2]
    nb = S // Q_BLOCK
    blocks = tuple(jnp.moveaxis(q.reshape(B, nb, Q_BLOCK, *q.shape[2:]), 1, 0) for q in qs)
    out = lax.map(lambda blk: fn(*blk), blocks)
    return jnp.moveaxis(out, 0, 1).reshape(B, S, *out.shape[3:])


def mla_attend(qn, qr, kn, kr, v):
    s = (jnp.einsum('bqhd,bkhd->bhqk', qn, kn) + jnp.einsum('bqhr,bkr->bhqk', qr, kr)).astype(jnp.float32)
    p = jax.nn.softmax(s * MLA_SCALE, axis=-1).astype(v.dtype)
    return jnp.einsum('bhqk,bkhd->bqhd', p, v)


def diff_attend(q, k, v, lam):
    B, Lq, H2, dh = q.shape
    s = jnp.einsum('bqhd,bkhd->bhqk', q, k).astype(jnp.float32) * (dh ** -0.5)
    p = jax.nn.softmax(s, axis=-1).reshape(B, H2 // 2, 2, Lq, -1)
    a = (p[:, :, 0] - lam * p[:, :, 1]).astype(v.dtype)
    return jnp.einsum('bhqk,bkhd->bqhd', a, v)


def diff_finish(o, norm_w, lam_init):
    B, L = o.shape[:2]
    return (rms_norm(o, norm_w) * (1.0 - lam_init)).reshape(B, L, BRANCH_B)


def swa_latent(q, k, v, kc, vc, sink):
    B, S, Hq, dh = q.shape
    Hkv = k.shape[2]
    G = Hq // Hkv
    nb = S // SWA_BLOCK
    qb = q.reshape(B, nb, SWA_BLOCK, Hkv, G, dh)

    def band(t):
        tp = jnp.pad(t, ((0, 0), (SWA_BLOCK, SWA_BLOCK), (0, 0), (0, 0)))
        tp = tp.reshape(B, nb + 2, SWA_BLOCK, *t.shape[2:])
        return jnp.concatenate([tp[:, :-2], tp[:, 1:-1], tp[:, 2:]], axis=2)

    kb, vb = band(k), band(v)
    scale = dh ** -0.5
    s_loc = jnp.einsum('bnqhgd,bnkhd->bnhgqk', qb, kb).astype(jnp.float32) * scale
    qpos = jnp.arange(S).reshape(nb, SWA_BLOCK)
    kpos = (jnp.arange(nb)[:, None] - 1) * SWA_BLOCK + jnp.arange(3 * SWA_BLOCK)[None, :]
    kp = kpos[:, None, :]
    mask = (jnp.abs(qpos[:, :, None] - kp) <= WINDOW) & (kp >= 0) & (kp < S)
    s_loc = jnp.where(mask[None, :, None, None], s_loc, -jnp.inf)
    s_ctx = jnp.einsum('bnqhgd,bkhd->bnhgqk', qb, kc).astype(jnp.float32) * scale
    sink_col = jnp.broadcast_to(sink.reshape(Hkv, G)[None, None, :, :, None, None].astype(jnp.float32),
                                s_loc.shape[:-1] + (1,))
    p = jax.nn.softmax(jnp.concatenate([s_loc, s_ctx, sink_col], axis=-1), axis=-1).astype(v.dtype)
    nloc = 3 * SWA_BLOCK
    nctx = kc.shape[1]
    o = (jnp.einsum('bnhgqk,bnkhd->bnqhgd', p[..., :nloc], vb)
         + jnp.einsum('bnhgqk,bkhd->bnqhgd', p[..., nloc:nloc + nctx], vc))
    return o.reshape(B, S, Hq * dh)


def swa_context(q, kc, vc, sink):
    B, C, Hq, dh = q.shape
    Hkv = kc.shape[2]
    G = Hq // Hkv
    qg = q.reshape(B, C, Hkv, G, dh)
    s = jnp.einsum('bqhgd,bkhd->bhgqk', qg, kc).astype(jnp.float32) * (dh ** -0.5)
    sink_col = jnp.broadcast_to(sink.reshape(Hkv, G)[None, :, :, None, None].astype(jnp.float32),
                                s.shape[:-1] + (1,))
    p = jax.nn.softmax(jnp.concatenate([s, sink_col], axis=-1), axis=-1)[..., :-1].astype(vc.dtype)
    return jnp.einsum('bhgqk,bkhd->bqhgd', p, vc).reshape(B, C, Hq * dh)


def latent_branches(qx, kx, kc, lam, lam_init, diff_norm_w, sink):
    B, S = qx['swa_q'].shape[:2]
    cat = lambda n: jnp.concatenate([kc[n], kx[n]], axis=1)
    mk, mkr, mv, dk, dv = cat('mla_k'), cat('mla_kr'), cat('mla_v'), cat('diff_k'), cat('diff_v')
    o_a = sweep_query_blocks(lambda qn, qr: mla_attend(qn, qr, mk, mkr, mv), qx['mla_qn'], qx['mla_qr'])
    o_b = sweep_query_blocks(lambda q: diff_attend(q, dk, dv, lam), qx['diff_q'])
    o_c = swa_latent(qx['swa_q'], kx['swa_k'], kx['swa_v'], kc['swa_k'], kc['swa_v'], sink)
    return o_a.reshape(B, S, BRANCH_A), diff_finish(o_b, diff_norm_w, lam_init), o_c


def context_branches(qc, kc, lam, lam_init, diff_norm_w, sink):
    B, C = qc['swa_q'].shape[:2]
    o_a = mla_attend(qc['mla_qn'], qc['mla_qr'], kc['mla_k'], kc['mla_kr'], kc['mla_v'])
    o_b = diff_attend(qc['diff_q'], kc['diff_k'], kc['diff_v'], lam)
    o_c = swa_context(qc['swa_q'], kc['swa_k'], kc['swa_v'], sink)
    return o_a.reshape(B, C, BRANCH_A), diff_finish(o_b, diff_norm_w, lam_init), o_c


def merge_branches(o_a, o_b, o_c, gates, wa, wb, wc, wo):
    ga, gb, gc = jnp.split(gates, N_BRANCH, axis=-1)
    return (ga * (o_a @ wa) + gb * (o_b @ wb) + gc * (o_c @ wc)) @ wo


def swiglu(h, w_gate_up, w_down):
    g, u = jnp.split(h @ w_gate_up, 2, axis=-1)
    return (jax.nn.silu(g) * u) @ w_down


def setup_inputs(seed: int = 0) -> dict:
    key = jax.random.key(seed)
    ks = iter(jax.random.split(key, 32))
    L, D = DEPTH, D_MODEL

    def nrm(shape, std=1.0):
        return jax.random.normal(next(ks), shape, jnp.float32) * std

    def w(shape, fan_in, gain=1.0):
        return nrm(shape, gain * fan_in ** -0.5)

    def gain_vec(shape):
        return 1.0 + nrm(shape, 0.05)

    return {
        'x': nrm((BATCH, SEQ, D)),
        'c': nrm((BATCH, D)),
        'ctx': nrm((BATCH, CTX_LEN, D)),
        'c_ctx': nrm((D,)),
        'w_ada': w((L, D, 6 * D), D, 0.5),
        'b_ada': nrm((L, 6 * D), 0.05),
        'attn_pre_norm': gain_vec((L, D)),
        'attn_post_norm': gain_vec((L, D)),
        'ffn_pre_norm': gain_vec((L, D)),
        'ffn_post_norm': gain_vec((L, D)),
        'w_in': w((L, D, IN_WIDTH), D),
        'mla_q_norm': gain_vec((L, MLA_Q_RANK)),
        'w_uq': w((L, MLA_Q_RANK, MLA_HEADS * (MLA_NOPE + MLA_ROPE)), MLA_Q_RANK),
        'mla_kv_norm': gain_vec((L, MLA_KV_RANK)),
        'w_ukv': w((L, MLA_KV_RANK, MLA_HEADS * (MLA_NOPE + MLA_V)), MLA_KV_RANK),
        'diff_lambda': nrm((L, 4, DIFF_DH), 0.1),
        'diff_norm': gain_vec((L, 2 * DIFF_DH)),
        'swa_sink': nrm((L, SWA_HEADS), 0.5),
        'w_branch_a': w((L, BRANCH_A, D), BRANCH_A),
        'w_branch_b': w((L, BRANCH_B, D), BRANCH_B),
        'w_branch_c': w((L, BRANCH_C, D), BRANCH_C),
        'w_o': w((L, D, D), D),
        'w_gate_up': w((L, D, 2 * D_FF), D),
        'w_down': w((L, D_FF, D), D_FF),
    }


def reference(x, c, ctx, c_ctx, w_ada, b_ada, attn_pre_norm, attn_post_norm, ffn_pre_norm,
              ffn_post_norm, w_in, mla_q_norm, w_uq, mla_kv_norm, w_ukv, diff_lambda, diff_norm,
              swa_sink, w_branch_a, w_branch_b, w_branch_c, w_o, w_gate_up, w_down):
    S = x.shape[1]
    rows = S // GRID_W
    tabs = {d: axial_rope_tables(rows, d) for d in sorted({MLA_ROPE, DIFF_DH, SWA_DH})}
    for l in range(DEPTH):
        last = l == DEPTH - 1
        lam_init = 0.8 - 0.6 * math.exp(-0.3 * l)
        lq1, lk1, lq2, lk2 = (diff_lambda[l, i].astype(jnp.float32) for i in range(4))
        lam = jnp.exp(jnp.sum(lq1 * lk1)) - jnp.exp(jnp.sum(lq2 * lk2)) + lam_init
        mx = adaln(c[:, None, :], w_ada[l], b_ada[l])
        mc = adaln(c_ctx[None, None, :], w_ada[l], b_ada[l])
        w_kv, w_q = w_in[l][:, :KV_WIDTH], w_in[l][:, KV_WIDTH:]

        hx = modulate(rms_norm(x, attn_pre_norm[l]), mx[0], mx[1])
        hc = modulate(rms_norm(ctx, attn_pre_norm[l]), mc[0], mc[1])
        kx = project_kv(hx, w_kv, mla_kv_norm[l], w_ukv[l], tabs)
        kc = project_kv(hc, w_kv, mla_kv_norm[l], w_ukv[l], None)
        qx = project_q(hx, w_q, mla_q_norm[l], w_uq[l], tabs)
        o_a, o_b, o_c = latent_branches(qx, kx, kc, lam, lam_init, diff_norm[l], swa_sink[l])
        y = merge_branches(o_a, o_b, o_c, qx['gates'], w_branch_a[l], w_branch_b[l], w_branch_c[l], w_o[l])
        x = x + mx[2] * rms_norm(y, attn_post_norm[l])
        if not last:
            qc = project_q(hc, w_q, mla_q_norm[l], w_uq[l], None)
            p_a, p_b, p_c = context_branches(qc, kc, lam, lam_init, diff_norm[l], swa_sink[l])
            yc = merge_branches(p_a, p_b, p_c, qc['gates'], w_branch_a[l], w_branch_b[l], w_branch_c[l], w_o[l])
            ctx = ctx + mc[2] * rms_norm(yc, attn_post_norm[l])

        fx = swiglu(modulate(rms_norm(x, ffn_pre_norm[l]), mx[3], mx[4]), w_gate_up[l], w_down[l])
        x = x + mx[5] * rms_norm(fx, ffn_post_norm[l])
        if not last:
            fc = swiglu(modulate(rms_norm(ctx, ffn_pre_norm[l]), mc[3], mc[4]), w_gate_up[l], w_down[l])
            ctx = ctx + mc[5] * rms_norm(fc, ffn_post_norm[l])
    return x
```

```python
import functools
import math

import jax
import jax.numpy as jnp
import numpy as np
from jax import lax
from jax.experimental import pallas as pl
from jax.experimental.pallas import tpu as pltpu

D_MODEL = 1024
GRID_W = 64
EPS = 1e-6
ROPE_BASE = 10000.0

MLA_HEADS = 8
MLA_RANK = 256
MLA_NOPE = 64
MLA_ROPE = 32
MLA_V = 64
MLA_SCALE = (MLA_NOPE + MLA_ROPE) ** -0.5

DIFF_HEADS = 4
DIFF_DH = 64

SWA_HEADS = 8
SWA_KV_HEADS = 2
SWA_DH = 64
SWA_GROUP = SWA_HEADS // SWA_KV_HEADS
WINDOW = 128
SWA_SCALE = SWA_DH ** -0.5
DIFF_SCALE = DIFF_DH ** -0.5

D_FF = 2816
N_MODS = 6

LANES = 128
VMEM_LIMIT = 56 * 1024 * 1024

_COLS = {}
_off = 0
for _name, _width in (('ckv', 256), ('cq', 256), ('kr', 128), ('dk', 512), ('sk', 128),
                      ('dq', 512), ('sq', 512), ('dv', 512), ('sv', 128), ('gates', 3072)):
    _COLS[_name] = (_off, _off + _width)
    _off += _width
IN_COLS = _off
SWA_HEAD_PERM = (0, 4, 1, 5, 2, 6, 3, 7)

BF16 = jnp.bfloat16
F32 = jnp.float32


def _dot(a, b):
    return jnp.dot(a, b, preferred_element_type=F32)


def _dot_nt(a, b):
    return lax.dot_general(a, b, (((1,), (1,)), ((), ())), preferred_element_type=F32)


def _rms(x, w):
    return x * lax.rsqrt(jnp.mean(x * x, axis=-1, keepdims=True) + EPS) * w


def _rope(x, cos, sin, first, shift):
    partner = jnp.where(first, pltpu.roll(x, LANES - shift, 1), pltpu.roll(x, shift, 1))
    return x * cos + partner * sin


def _const_spec(shape):
    nd = len(shape)
    return pl.BlockSpec(shape, lambda *_: (0,) * nd, pipeline_mode=pl.Buffered(1))


def _params(n_axes):
    return pltpu.CompilerParams(dimension_semantics=('arbitrary',) * n_axes,
                                vmem_limit_bytes=VMEM_LIMIT)


def _adaln_kernel(c_ref, w_ref, b_ref, o_ref):
    c = c_ref[...]
    s = c * jax.nn.sigmoid(c)
    o_ref[...] = _dot(s.astype(BF16), w_ref[...].astype(BF16)) + b_ref[...]


def _adaln(cond, w, b):
    rows, d = cond.shape
    n = w.shape[1]
    tn = n // 4
    return pl.pallas_call(
        _adaln_kernel,
        grid=(n // tn,),
        in_specs=[pl.BlockSpec((rows, d), lambda j: (0, 0)),
                  pl.BlockSpec((d, tn), lambda j: (0, j)),
                  pl.BlockSpec((1, tn), lambda j: (0, j))],
        out_specs=pl.BlockSpec((rows, tn), lambda j: (0, j)),
        out_shape=jax.ShapeDtypeStruct((rows, n), F32),
        compiler_params=_params(1),
        name='adaln',
    )(cond, w, b.reshape(1, n))


def _in_proj_kernel(x_ref, mod_ref, pre_ref, cos64_ref, sin64_ref, cosm_ref, sinm_ref,
                    w_ref, kvn_ref, wuk_ref, wuv_ref, qn_ref, wuq_ref, *outs, rope, with_q):
    x = x_ref[0]
    mod = mod_ref[0]
    h = _rms(x, pre_ref[...]) * (1.0 + mod[1:2]) + mod[0:1]
    hb = h.astype(BF16)

    def proj(name):
        a, b = _COLS[name]
        return _dot(hb, w_ref[:, a:b])

    lane = lax.broadcasted_iota(jnp.int32, (x.shape[0], LANES), 1)
    first64 = (lane & 31) < 16
    firstm = (lane & 15) < 8
    if rope:
        cos64, sin64, cosm, sinm = cos64_ref[...], sin64_ref[...], cosm_ref[...], sinm_ref[...]

    def rope64(z, scale):
        if not rope:
            return z if scale == 1.0 else z * scale
        c, s = (cos64, sin64) if scale == 1.0 else (cos64 * scale, sin64 * scale)
        return jnp.concatenate(
            [_rope(z[:, i:i + LANES], c, s, first64, 16) for i in range(0, z.shape[1], LANES)], axis=1)

    (mk_ref, mv_ref, dk_ref, dv_ref, sk_ref, sv_ref), q_outs = outs[:6], outs[6:]

    ckv = _rms(proj('ckv'), kvn_ref[...]).astype(BF16)
    k_nope = _dot(ckv, wuk_ref[...])
    kr = proj('kr')
    if rope:
        kr = _rope(kr, cosm, sinm, firstm, 8)
    for i in range(0, MLA_HEADS * LANES, LANES):
        mk_ref[0, :, i:i + LANES] = (k_nope[:, i:i + LANES] + kr).astype(BF16)
    mv_ref[0] = _dot(ckv, wuv_ref[...]).astype(BF16)

    dk_ref[0] = rope64(proj('dk'), 1.0).astype(BF16)
    dv_ref[0] = proj('dv').astype(BF16)
    sk_ref[0] = rope64(proj('sk'), 1.0).astype(BF16)
    sv_ref[0] = proj('sv').astype(BF16)

    if with_q:
        mq_ref, dq_ref, sq_ref, g_ref = q_outs
        cq = _rms(proj('cq'), qn_ref[...]).astype(BF16)
        q = _dot(cq, wuq_ref[...])
        for i in range(0, MLA_HEADS * LANES, LANES):
            qi = q[:, i:i + LANES]
            if rope:
                qi = _rope(qi, cosm * MLA_SCALE, sinm * MLA_SCALE, firstm, 8)
            else:
                qi = qi * MLA_SCALE
            mq_ref[0, :, i:i + LANES] = qi.astype(BF16)
        dq_ref[0] = rope64(proj('dq'), DIFF_SCALE).astype(BF16)
        sq_ref[0] = rope64(proj('sq'), SWA_SCALE).astype(BF16)
        a, b = _COLS['gates']
        for i in range(a, b, 512):
            g_ref[0, :, i - a:i - a + 512] = jax.nn.sigmoid(_dot(hb, w_ref[:, i:i + 512])).astype(BF16)


def _in_proj(x, mods, pre_w, tabs, wcat, kvn, wuk, wuv, qn, wuq, *, rope, with_q, tm):
    bsz, n, d = x.shape
    per_batch_mod = mods.shape[0] > 1
    widths = [MLA_HEADS * LANES, MLA_HEADS * MLA_V, 512, 512, 128, 128]
    if with_q:
        widths += [MLA_HEADS * LANES, 512, 512, 3 * D_MODEL]
    row_spec = lambda w: pl.BlockSpec((1, tm, w), lambda b, i: (b, i, 0))
    tab_spec = pl.BlockSpec((tm, LANES), lambda b, i: (i, 0))
    kern = functools.partial(_in_proj_kernel, rope=rope, with_q=with_q)
    return pl.pallas_call(
        kern,
        grid=(bsz, n // tm),
        in_specs=[row_spec(d),
                  pl.BlockSpec((1, N_MODS, d), (lambda b, i: (b, 0, 0)) if per_batch_mod else (lambda b, i: (0, 0, 0))),
                  _const_spec((1, d)),
                  tab_spec, tab_spec, tab_spec, tab_spec,
                  _const_spec(wcat.shape), _const_spec(kvn.shape), _const_spec(wuk.shape),
                  _const_spec(wuv.shape), _const_spec(qn.shape), _const_spec(wuq.shape)],
        out_specs=[row_spec(w) for w in widths],
        out_shape=[jax.ShapeDtypeStruct((bsz, n, w), BF16) for w in widths],
        compiler_params=_params(2),
        name='in_proj_rope' if rope else 'in_proj_ctx',
    )(x, mods, pre_w, *tabs, wcat, kvn, wuk, wuv, qn, wuq)


def _scores(q, k_refs, col):
    return [_dot_nt(q, k_ref[0, :, col:col + LANES]) for k_ref in k_refs]


def _softmax_parts(ss, extra=None):
    m = functools.reduce(jnp.maximum, [jnp.max(s, axis=-1, keepdims=True) for s in ss])
    if extra is not None:
        m = jnp.maximum(m, extra)
    es = [jnp.exp(s - m) for s in ss]
    l = functools.reduce(jnp.add, [jnp.sum(e, axis=-1, keepdims=True) for e in es])
    if extra is not None:
        l = l + jnp.exp(extra - m)
    return es, l


def _mla_kernel(*refs, n_seg):
    q_ref = refs[0]
    k_refs = refs[1:1 + n_seg]
    v_refs = refs[1 + n_seg:1 + 2 * n_seg]
    o_ref = refs[1 + 2 * n_seg]
    outs = []
    for j in range(2):
        q = q_ref[0, :, j * LANES:(j + 1) * LANES]
        es, l = _softmax_parts(_scores(q, k_refs, j * LANES))
        o = functools.reduce(jnp.add, [_dot(e.astype(BF16), v_ref[0]) for e, v_ref in zip(es, v_refs)])
        outs.append(o / l)
    lane = lax.broadcasted_iota(jnp.int32, outs[0].shape, 1)
    o_ref[0] = jnp.where(lane < MLA_V, outs[0], outs[1]).astype(BF16)


def _mla(q, ks, vs, *, tq):
    bsz, n, _ = q.shape
    n_seg = len(ks)
    kv_spec = lambda a, w: pl.BlockSpec((1, a.shape[1], w), lambda b, h, i: (b, 0, h))
    return pl.pallas_call(
        functools.partial(_mla_kernel, n_seg=n_seg),
        grid=(bsz, MLA_HEADS // 2, n // tq),
        in_specs=[pl.BlockSpec((1, tq, 2 * LANES), lambda b, h, i: (b, i, h))]
                 + [kv_spec(k, 2 * LANES) for k in ks] + [kv_spec(v, LANES) for v in vs],
        out_specs=pl.BlockSpec((1, tq, LANES), lambda b, h, i: (b, i, h)),
        out_shape=jax.ShapeDtypeStruct((bsz, n, MLA_HEADS * MLA_V), BF16),
        compiler_params=_params(3),
        name=f'mla_attn_{n_seg}seg',
    )(q, *ks, *vs)


def _diff_kernel(*refs, n_seg, lam_init):
    q_ref, lam_ref, nrm_ref = refs[:3]
    k_refs = refs[3:3 + n_seg]
    v_refs = refs[3 + n_seg:3 + 2 * n_seg]
    o_ref = refs[3 + 2 * n_seg]
    lv = lam_ref[...]
    lam = (jnp.exp(jnp.sum(lv[0:1] * lv[1:2], axis=-1, keepdims=True))
           - jnp.exp(jnp.sum(lv[2:3] * lv[3:4], axis=-1, keepdims=True)) + lam_init)
    q = q_ref[0]
    lane = lax.broadcasted_iota(jnp.int32, q.shape, 1)
    zero = jnp.zeros_like(q)
    e0, l0 = _softmax_parts(_scores(jnp.where(lane < DIFF_DH, q, zero), k_refs, 0))
    e1, l1 = _softmax_parts(_scores(jnp.where(lane >= DIFF_DH, q, zero), k_refs, 0))
    r0 = 1.0 / l0
    r1 = lam / l1
    o = functools.reduce(jnp.add, [_dot((a * r0 - b * r1).astype(BF16), v_ref[0])
                                   for a, b, v_ref in zip(e0, e1, v_refs)])
    o_ref[0] = (_rms(o, nrm_ref[...]) * (1.0 - lam_init)).astype(BF16)


def _diff(q, lam_w, nrm_w, ks, vs, *, lam_init, tq):
    bsz, n, _ = q.shape
    n_seg = len(ks)
    kv_spec = lambda a: pl.BlockSpec((1, a.shape[1], LANES), lambda b, h, i: (b, 0, h))
    return pl.pallas_call(
        functools.partial(_diff_kernel, n_seg=n_seg, lam_init=lam_init),
        grid=(bsz, DIFF_HEADS, n // tq),
        in_specs=[pl.BlockSpec((1, tq, LANES), lambda b, h, i: (b, i, h)),
                  _const_spec(lam_w.shape), _const_spec(nrm_w.shape)]
                 + [kv_spec(k) for k in ks] + [kv_spec(v) for v in vs],
        out_specs=pl.BlockSpec((1, tq, LANES), lambda b, h, i: (b, i, h)),
        out_shape=jax.ShapeDtypeStruct((bsz, n, DIFF_HEADS * 2 * DIFF_DH), BF16),
        compiler_params=_params(3),
        name=f'diff_attn_{n_seg}seg',
    )(q, lam_w, nrm_w, *ks, *vs)


def _swa_kernel(*refs, local, seq):
    if local:
        q_ref, sink_ref, kc_ref, vc_ref, kx_ref, vx_ref, o_ref = refs
    else:
        q_ref, sink_ref, kc_ref, vc_ref, o_ref = refs
    blk = WINDOW
    rows = SWA_GROUP * blk
    lane = lax.broadcasted_iota(jnp.int32, (blk, LANES), 1)
    kc, vc = kc_ref[0], vc_ref[0]
    if local:
        i = pl.program_id(1)
        start = pl.multiple_of(jnp.clip((i - 1) * blk, 0, seq - 3 * blk), blk)
        kl = kx_ref[0, pl.ds(start, 3 * blk), :]
        vl = vx_ref[0, pl.ds(start, 3 * blk), :]
        qpos = i * blk + (lax.broadcasted_iota(jnp.int32, (rows, 3 * blk), 0) & (blk - 1))
        kpos = start + lax.broadcasted_iota(jnp.int32, (rows, 3 * blk), 1)
        visible = jnp.abs(qpos - kpos) <= WINDOW
    halves = []
    for half in range(SWA_KV_HEADS):
        keep = (lane < SWA_DH) if half == 0 else (lane >= SWA_DH)
        q = jnp.concatenate(
            [jnp.where(keep, q_ref[0, :, c * LANES:(c + 1) * LANES], jnp.zeros((blk, LANES), BF16))
             for c in range(SWA_GROUP)], axis=0)
        sink = sink_ref[half][:, 0:1]
        ss = [_dot_nt(q, kc)]
        vals = [vc]
        if local:
            ss.append(jnp.where(visible, _dot_nt(q, kl), -jnp.inf))
            vals.append(vl)
        es, l = _softmax_parts(ss, extra=sink)
        o = functools.reduce(jnp.add, [_dot(e.astype(BF16), v) for e, v in zip(es, vals)])
        halves.append(o / l)
    out = jnp.where(lax.broadcasted_iota(jnp.int32, (rows, LANES), 1) < SWA_DH, halves[0], halves[1])
    for c in range(SWA_GROUP):
        o_ref[0, :, c * LANES:(c + 1) * LANES] = out[c * blk:(c + 1) * blk].astype(BF16)


def _swa(q, sink_rows, kc, vc, kx=None, vx=None):
    bsz, n, _ = q.shape
    local = kx is not None
    full = lambda a: pl.BlockSpec((1,) + a.shape[1:], lambda b, i: (b, 0, 0))
    kv = [kc, vc] + ([kx, vx] if local else [])
    return pl.pallas_call(
        functools.partial(_swa_kernel, local=local, seq=n),
        grid=(bsz, n // WINDOW),
        in_specs=[pl.BlockSpec((1, WINDOW, SWA_HEADS * SWA_DH), lambda b, i: (b, i, 0)),
                  _const_spec(sink_rows.shape)] + [full(a) for a in kv],
        out_specs=pl.BlockSpec((1, WINDOW, SWA_HEADS * SWA_DH), lambda b, i: (b, i, 0)),
        out_shape=jax.ShapeDtypeStruct((bsz, n, SWA_HEADS * SWA_DH), BF16),
        compiler_params=_params(2),
        name='swa_attn_local' if local else 'swa_attn_ctx',
    )(q, sink_rows, *kv)


def _merge_kernel(x_ref, mod_ref, post_ref, oa_ref, ob_ref, oc_ref, g_ref,
                  wa_ref, wb_ref, wc_ref, wo_ref, o_ref):
    d = D_MODEL
    y = (g_ref[0, :, 0:d].astype(F32) * _dot(oa_ref[0], wa_ref[...])
         + g_ref[0, :, d:2 * d].astype(F32) * _dot(ob_ref[0], wb_ref[...])
         + g_ref[0, :, 2 * d:3 * d].astype(F32) * _dot(oc_ref[0], wc_ref[...]))
    y = _dot(y.astype(BF16), wo_ref[...])
    mod = mod_ref[0]
    o_ref[0] = x_ref[0] + mod[2:3] * _rms(y, post_ref[...])


def _merge(x, mods, post_w, oa, ob, oc, gates, wa, wb, wc, wo, *, tm):
    bsz, n, d = x.shape
    per_batch_mod = mods.shape[0] > 1
    row_spec = lambda w: pl.BlockSpec((1, tm, w), lambda b, i: (b, i, 0))
    return pl.pallas_call(
        _merge_kernel,
        grid=(bsz, n // tm),
        in_specs=[row_spec(d),
                  pl.BlockSpec((1, N_MODS, d), (lambda b, i: (b, 0, 0)) if per_batch_mod else (lambda b, i: (0, 0, 0))),
                  _const_spec((1, d)),
                  row_spec(512), row_spec(512), row_spec(512), row_spec(3 * d),
                  _const_spec(wa.shape), _const_spec(wb.shape), _const_spec(wc.shape), _const_spec(wo.shape)],
        out_specs=row_spec(d),
        out_shape=jax.ShapeDtypeStruct((bsz, n, d), F32),
        compiler_params=_params(2),
        name='merge_out_proj',
    )(x, mods, post_w, oa, ob, oc, gates, wa, wb, wc, wo)


def _ffn_kernel(x_ref, mod_ref, pre_ref, post_ref, wgu_ref, wd_ref, o_ref):
    x = x_ref[0]
    mod = mod_ref[0]
    h = (_rms(x, pre_ref[...]) * (1.0 + mod[4:5]) + mod[3:4]).astype(BF16)
    g = _dot(h, wgu_ref[:, 0:D_FF])
    u = _dot(h, wgu_ref[:, D_FF:2 * D_FF])
    act = (g * jax.nn.sigmoid(g) * u).astype(BF16)
    f = _dot(act, wd_ref[...])
    o_ref[0] = x + mod[5:6] * _rms(f, post_ref[...])


def _ffn(x, mods, pre_w, post_w, wgu, wd, *, tm):
    bsz, n, d = x.shape
    per_batch_mod = mods.shape[0] > 1
    row_spec = pl.BlockSpec((1, tm, d), lambda b, i: (b, i, 0))
    return pl.pallas_call(
        _ffn_kernel,
        grid=(bsz, n // tm),
        in_specs=[row_spec,
                  pl.BlockSpec((1, N_MODS, d), (lambda b, i: (b, 0, 0)) if per_batch_mod else (lambda b, i: (0, 0, 0))),
                  _const_spec((1, d)), _const_spec((1, d)),
                  _const_spec(wgu.shape), _const_spec(wd.shape)],
        out_specs=row_spec,
        out_shape=jax.ShapeDtypeStruct((bsz, n, d), F32),
        compiler_params=_params(2),
        name='swiglu',
    )(x, mods, pre_w, post_w, wgu, wd)


def _rope_tables(seq):
    t = jnp.arange(seq)
    pos_row = (t // GRID_W).astype(F32)[:, None]
    pos_col = (t % GRID_W).astype(F32)[:, None]
    lanes = np.arange(LANES)

    def table(rot_dim, slot):
        axis_dim = rot_dim // 2
        half = axis_dim // 2
        inv = ROPE_BASE ** (-jnp.arange(0, axis_dim, 2, dtype=F32) / axis_dim)
        valid = slot >= 0
        axis, j = np.divmod(np.where(valid, slot, 0), axis_dim)
        ang = jnp.where(axis[None, :] == 0, pos_row, pos_col) * inv[j % half][None, :]
        sign = np.where(j < half, -1.0, 1.0).astype(np.float32)
        cos = jnp.where(valid[None, :], jnp.cos(ang), 1.0)
        sin = jnp.where(valid[None, :], jnp.sin(ang) * sign[None, :], 0.0)
        return cos, sin

    cos64, sin64 = table(SWA_DH, lanes % SWA_DH)
    in_rope = (lanes >= MLA_NOPE) & (lanes < MLA_NOPE + MLA_ROPE)
    cosm, sinm = table(MLA_ROPE, np.where(in_rope, lanes - MLA_NOPE, -1))
    return cos64, sin64, cosm, sinm


def _pack_in_proj(w):
    d = w.shape[0]
    ckv, kr, dk, dv, sk, sv, cq, dq, sq, gates = jnp.split(
        w, np.cumsum([256, 32, 512, 512, 128, 128, 256, 512, 512]).tolist(), axis=1)
    kr_wide = jnp.zeros((d, LANES), w.dtype).at[:, MLA_NOPE:MLA_NOPE + MLA_ROPE].set(kr)
    sq = sq.reshape(d, SWA_HEADS, SWA_DH)[:, np.array(SWA_HEAD_PERM)].reshape(d, SWA_HEADS * SWA_DH)
    parts = dict(ckv=ckv, cq=cq, kr=kr_wide, dk=dk, sk=sk, dq=dq, sq=sq, dv=dv, sv=sv, gates=gates)
    return jnp.concatenate([parts[name] for name in _COLS], axis=1).astype(BF16)


def _pad_heads(w, used):
    r = w.shape[0]
    w = w.reshape(r, MLA_HEADS, used)
    return jnp.pad(w, ((0, 0), (0, 0), (0, LANES - used))).reshape(r, MLA_HEADS * LANES)


def kernel(x, c, ctx, c_ctx, w_ada, b_ada, attn_pre_norm, attn_post_norm, ffn_pre_norm, ffn_post_norm, w_in, mla_q_norm, w_uq, mla_kv_norm, w_ukv, diff_lambda, diff_norm, swa_sink, w_branch_a, w_branch_b, w_branch_c, w_o, w_gate_up, w_down):
    bsz, seq, d = x.shape
    depth = w_ada.shape[0]
    tabs = _rope_tables(seq)
    cond = jnp.concatenate([c, c_ctx[None, :], jnp.zeros((8 - bsz - 1, d), c.dtype)], axis=0)
    row = lambda v: v.reshape(1, -1)
    tm_ctx = ctx.shape[1]

    for l in range(depth):
        last = l == depth - 1
        lam_init = 0.8 - 0.6 * math.exp(-0.3 * l)
        mods = _adaln(cond, w_ada[l], b_ada[l])
        mx = mods[:bsz].reshape(bsz, N_MODS, d)
        mc = mods[bsz:bsz + 1].reshape(1, N_MODS, d)

        wcat = _pack_in_proj(w_in[l])
        ukv = w_ukv[l].reshape(MLA_RANK, MLA_HEADS, MLA_NOPE + MLA_V)
        wuk = _pad_heads(ukv[:, :, :MLA_NOPE].reshape(MLA_RANK, -1), MLA_NOPE).astype(BF16)
        wuv = ukv[:, :, MLA_NOPE:].reshape(MLA_RANK, MLA_HEADS * MLA_V).astype(BF16)
        wuq = _pad_heads(w_uq[l], MLA_NOPE + MLA_ROPE).astype(BF16)
        proj_w = (wcat, row(mla_kv_norm[l]), wuk, wuv, row(mla_q_norm[l]), wuq)
        sink_rows = jnp.broadcast_to(
            jnp.repeat(swa_sink[l].reshape(SWA_KV_HEADS, SWA_GROUP), WINDOW, axis=1)[:, :, None],
            (SWA_KV_HEADS, SWA_GROUP * WINDOW, LANES))
        wa, wb = w_branch_a[l].astype(BF16), w_branch_b[l].astype(BF16)
        wc = w_branch_c[l].reshape(SWA_HEADS, SWA_DH, d)[np.array(SWA_HEAD_PERM)].reshape(-1, d).astype(BF16)
        wo, wgu, wd = w_o[l].astype(BF16), w_gate_up[l].astype(BF16), w_down[l].astype(BF16)
        dnorm = row(diff_norm[l])
        pre_a, post_a = row(attn_pre_norm[l]), row(attn_post_norm[l])
        pre_f, post_f = row(ffn_pre_norm[l]), row(ffn_post_norm[l])

        px = _in_proj(x, mx, pre_a, tabs, *proj_w, rope=True, with_q=True, tm=256)
        pc = _in_proj(ctx, mc, pre_a, tabs, *proj_w, rope=False, with_q=not last, tm=tm_ctx)
        mkx, mvx, dkx, dvx, skx, svx, mqx, dqx, sqx, gx = px
        mkc, mvc, dkc, dvc, skc, svc = pc[:6]
        o_a = _mla(mqx, [mkc, mkx], [mvc, mvx], tq=256)
        o_b = _diff(dqx, diff_lambda[l], dnorm, [dkc, dkx], [dvc, dvx], lam_init=lam_init, tq=256)
        o_c = _swa(sqx, sink_rows, skc, svc, skx, svx)
        x = _merge(x, mx, post_a, o_a, o_b, o_c, gx, wa, wb, wc, wo, tm=256)
        if not last:
            mqc, dqc, sqc, gc = pc[6:]
            p_a = _mla(mqc, [mkc], [mvc], tq=tm_ctx)
            p_b = _diff(dqc, diff_lambda[l], dnorm, [dkc], [dvc], lam_init=lam_init, tq=tm_ctx)
            p_c = _swa(sqc, sink_rows, skc, svc)
            ctx = _merge(ctx, mc, post_a, p_a, p_b, p_c, gc, wa, wb, wc, wo, tm=tm_ctx)

        x = _ffn(x, mx, pre_f, post_f, wgu, wd, tm=256)
        if not last:
            ctx = _ffn(ctx, mc, pre_f, post_f, wgu, wd, tm=tm_ctx)
    return x
```

```python
import functools
import math

import jax
import jax.numpy as jnp
import numpy as np
from jax import lax
from jax.experimental import pallas as pl
from jax.experimental.pallas import tpu as pltpu

D_MODEL = 1024
GRID_W = 64
EPS = 1e-6
ROPE_BASE = 10000.0
LOG2E = math.log2(math.e)

MLA_HEADS = 8
MLA_RANK = 256
MLA_NOPE = 64
MLA_ROPE = 32
MLA_V = 64
MLA_SCALE = (MLA_NOPE + MLA_ROPE) ** -0.5

DIFF_HEADS = 4
DIFF_DH = 64
DIFF_SCALE = DIFF_DH ** -0.5

SWA_HEADS = 8
SWA_KV_HEADS = 2
SWA_DH = 64
SWA_GROUP = SWA_HEADS // SWA_KV_HEADS
WINDOW = 128
SWA_SCALE = SWA_DH ** -0.5

D_FF = 2816
N_MODS = 6

LANES = 128
BF16_ROWS = 16
VMEM_LIMIT = 56 * 1024 * 1024

ROW_TILE = 256
Q_TILE = 512
Q_SUB = 256
KEY_CHUNK = 1024

_COLS = {}
_off = 0
for _name, _width in (('ckv', 256), ('cq', 256), ('kr', 128), ('dk', 512), ('sk', 128),
                      ('dq', 512), ('sq', 512), ('sv', 128), ('gates', 3072)):
    _COLS[_name] = (_off, _off + _width)
    _off += _width
IN_COLS = _off
SWA_HEAD_PERM = (0, 4, 1, 5, 2, 6, 3, 7)

BF16 = jnp.bfloat16
F32 = jnp.float32


def _dot(a, b):
    return jnp.dot(a, b, preferred_element_type=F32)


def _dot_nt(a, b):
    return lax.dot_general(a, b, (((1,), (1,)), ((), ())), preferred_element_type=F32)


def _rms(x, w):
    return x * lax.rsqrt(jnp.mean(x * x, axis=-1, keepdims=True) + EPS) * w


def _rope(x, cos, sin, first, shift):
    partner = jnp.where(first, pltpu.roll(x, LANES - shift, 1), pltpu.roll(x, shift, 1))
    return x * cos + partner * sin


def _const_spec(shape):
    nd = len(shape)
    return pl.BlockSpec(shape, lambda *_: (0,) * nd, pipeline_mode=pl.Buffered(1))


def _params(n_axes):
    return pltpu.CompilerParams(dimension_semantics=('arbitrary',) * n_axes,
                                vmem_limit_bytes=VMEM_LIMIT)


def _adaln_kernel(c_ref, w_ref, b_ref, o_ref):
    c = c_ref[...]
    s = c * jax.nn.sigmoid(c)
    o_ref[...] = _dot(s.astype(BF16), w_ref[...].astype(BF16)) + b_ref[...]


def _adaln(cond, w, b):
    rows, d = cond.shape
    n = w.shape[1]
    tn = n // 4
    return pl.pallas_call(
        _adaln_kernel,
        grid=(n // tn,),
        in_specs=[pl.BlockSpec((rows, d), lambda j: (0, 0)),
                  pl.BlockSpec((d, tn), lambda j: (0, j)),
                  pl.BlockSpec((1, tn), lambda j: (0, j))],
        out_specs=pl.BlockSpec((rows, tn), lambda j: (0, j)),
        out_shape=jax.ShapeDtypeStruct((rows, n), F32),
        compiler_params=_params(1),
        name='adaln',
    )(cond, w, b.reshape(1, n))


def _in_proj_kernel(x_ref, mod_ref, pre_ref, cos64_ref, sin64_ref, cosm_ref, sinm_ref,
                    w_ref, wdvt_ref, kvn_ref, wuk_ref, wuvt_ref, qn_ref, wuq_ref, *outs, rope, with_q):
    x = x_ref[0]
    mod = mod_ref[0]
    h = _rms(x, pre_ref[...]) * (1.0 + mod[1:2]) + mod[0:1]
    hb = h.astype(BF16)

    def proj(name):
        a, b = _COLS[name]
        return _dot(hb, w_ref[:, a:b])

    lane = lax.broadcasted_iota(jnp.int32, (x.shape[0], LANES), 1)
    first64 = (lane & 31) < 16
    firstm = (lane & 15) < 8
    if rope:
        cos64, sin64, cosm, sinm = cos64_ref[...], sin64_ref[...], cosm_ref[...], sinm_ref[...]

    def rope64(z, scale):
        if not rope:
            return z if scale == 1.0 else z * scale
        c, s = (cos64, sin64) if scale == 1.0 else (cos64 * scale, sin64 * scale)
        return jnp.concatenate(
            [_rope(z[:, i:i + LANES], c, s, first64, 16) for i in range(0, z.shape[1], LANES)], axis=1)

    (mk_ref, mvt_ref, dk_ref, dvt_ref, sk_ref, sv_ref), q_outs = outs[:6], outs[6:]

    ckv = _rms(proj('ckv'), kvn_ref[...]).astype(BF16)
    k_nope = _dot(ckv, wuk_ref[...])
    kr = proj('kr')
    if rope:
        kr = _rope(kr, cosm, sinm, firstm, 8)
    for i in range(0, MLA_HEADS * LANES, LANES):
        mk_ref[0, :, i:i + LANES] = (k_nope[:, i:i + LANES] + kr).astype(BF16)
    mvt_ref[0] = _dot_nt(wuvt_ref[...], ckv).astype(BF16)

    dk_ref[0] = rope64(proj('dk'), 1.0).astype(BF16)
    dvt_ref[0] = _dot_nt(wdvt_ref[...], hb).astype(BF16)
    sk_ref[0] = rope64(proj('sk'), 1.0).astype(BF16)
    sv_ref[0] = proj('sv').astype(BF16)

    if with_q:
        mq_ref, dq_ref, sq_ref, g_ref = q_outs
        cq = _rms(proj('cq'), qn_ref[...]).astype(BF16)
        q = _dot(cq, wuq_ref[...])
        mla_scale = MLA_SCALE * LOG2E
        for i in range(0, MLA_HEADS * LANES, LANES):
            qi = q[:, i:i + LANES]
            if rope:
                qi = _rope(qi, cosm * mla_scale, sinm * mla_scale, firstm, 8)
            else:
                qi = qi * mla_scale
            mq_ref[0, :, i:i + LANES] = qi.astype(BF16)
        dq_ref[0] = rope64(proj('dq'), DIFF_SCALE * LOG2E).astype(BF16)
        sq_ref[0] = rope64(proj('sq'), SWA_SCALE * LOG2E).astype(BF16)
        a, b = _COLS['gates']
        for i in range(a, b, 512):
            g_ref[0, :, i - a:i - a + 512] = jax.nn.sigmoid(_dot(hb, w_ref[:, i:i + 512])).astype(BF16)


def _in_proj(x, mods, pre_w, tabs, wcat, wdvt, kvn, wuk, wuvt, qn, wuq, *, rope, with_q):
    bsz, n, d = x.shape
    tm = min(ROW_TILE, n)
    per_batch_mod = mods.shape[0] > 1
    outs = [(MLA_HEADS * LANES, False), (MLA_HEADS * MLA_V, True), (512, False), (512, True),
            (128, False), (128, False)]
    if with_q:
        outs += [(MLA_HEADS * LANES, False), (512, False), (512, False), (3 * D_MODEL, False)]
    row_spec = lambda w: pl.BlockSpec((1, tm, w), lambda b, i: (b, i, 0))
    col_spec = lambda w: pl.BlockSpec((1, w, tm), lambda b, i: (b, 0, i))
    tab_spec = pl.BlockSpec((tm, LANES), lambda b, i: (i, 0))
    kern = functools.partial(_in_proj_kernel, rope=rope, with_q=with_q)
    return pl.pallas_call(
        kern,
        grid=(bsz, n // tm),
        in_specs=[row_spec(d),
                  pl.BlockSpec((1, N_MODS, d), (lambda b, i: (b, 0, 0)) if per_batch_mod else (lambda b, i: (0, 0, 0))),
                  _const_spec((1, d)),
                  tab_spec, tab_spec, tab_spec, tab_spec,
                  _const_spec(wcat.shape), _const_spec(wdvt.shape), _const_spec(kvn.shape),
                  _const_spec(wuk.shape), _const_spec(wuvt.shape), _const_spec(qn.shape), _const_spec(wuq.shape)],
        out_specs=[col_spec(w) if t else row_spec(w) for w, t in outs],
        out_shape=[jax.ShapeDtypeStruct((bsz, w, n) if t else (bsz, n, w), BF16) for w, t in outs],
        compiler_params=_params(2),
        name='in_proj_rope' if rope else 'in_proj_ctx',
    )(x, mods, pre_w, *tabs, wcat, wdvt, kvn, wuk, wuvt, qn, wuq)


def _kv_chunks(k_refs, vt_refs, col, vrow, dv):
    chunks = []
    for k_ref, vt_ref in zip(k_refs, vt_refs):
        n = k_ref.shape[1]
        step = min(KEY_CHUNK, n)
        for c in range(0, n, step):
            chunks.append((functools.partial(lambda r, c, s: r[0, c:c + s, col:col + LANES], k_ref, c, step),
                           functools.partial(lambda r, c, s: r[0, vrow:vrow + dv, c:c + s], vt_ref, c, step)))
    return chunks


def _flash_chains(chains):
    n = len(chains[0][1])
    state = [(None, None)] * len(chains)
    s_next = [_dot_nt(chunks[0][0](), q) for q, chunks in chains]
    for i in range(n):
        for ci, (q, chunks) in enumerate(chains):
            m, acc = state[ci]
            s = s_next[ci]
            if i + 1 < n:
                s_next[ci] = _dot_nt(chunks[i + 1][0](), q)
            mc = jnp.max(s, axis=0, keepdims=True)
            m_new = mc if m is None else jnp.maximum(m, mc)
            p = jnp.exp2(s - m_new).astype(BF16)
            vt = chunks[i][1]()
            vt = jnp.concatenate([vt, jnp.ones((BF16_ROWS, vt.shape[1]), BF16)], axis=0)
            pv = _dot(vt, p)
            acc = pv if m is None else jnp.exp2(m - m_new) * acc + pv
            state[ci] = (m_new, acc)
    return [acc for _, acc in state]


def _mla_kernel(*refs, n_seg):
    q_ref = refs[0]
    k_refs = refs[1:1 + n_seg]
    vt_refs = refs[1 + n_seg:1 + 2 * n_seg]
    o_ref = refs[1 + 2 * n_seg]
    tq = q_ref.shape[1]
    subs = range(0, tq, Q_SUB)
    chains = []
    for j in range(2):
        chunks = _kv_chunks(k_refs, vt_refs, j * LANES, j * MLA_V, MLA_V)
        chains += [(q_ref[0, r:r + Q_SUB, j * LANES:(j + 1) * LANES], chunks) for r in subs]
    res = _flash_chains(chains)
    for ri, r in enumerate(subs):
        pair = [res[j * len(subs) + ri] for j in range(2)]
        o_t = jnp.concatenate([a[:MLA_V] / a[MLA_V:MLA_V + 1] for a in pair], axis=0)
        o_ref[0, r:r + Q_SUB, :] = o_t.T.astype(BF16)


def _mla(q, ks, vts):
    bsz, n, _ = q.shape
    tq = min(Q_TILE, n)
    n_seg = len(ks)
    return pl.pallas_call(
        functools.partial(_mla_kernel, n_seg=n_seg),
        grid=(bsz, MLA_HEADS // 2, n // tq),
        in_specs=[pl.BlockSpec((1, tq, 2 * LANES), lambda b, h, i: (b, i, h))]
                 + [pl.BlockSpec((1, k.shape[1], 2 * LANES), lambda b, h, i: (b, 0, h)) for k in ks]
                 + [pl.BlockSpec((1, 2 * MLA_V, v.shape[2]), lambda b, h, i: (b, h, 0)) for v in vts],
        out_specs=pl.BlockSpec((1, tq, LANES), lambda b, h, i: (b, i, h)),
        out_shape=jax.ShapeDtypeStruct((bsz, n, MLA_HEADS * MLA_V), BF16),
        compiler_params=_params(3),
        name=f'mla_attn_{n_seg}seg',
    )(q, *ks, *vts)


def _diff_kernel(*refs, n_seg, lam_init):
    q_ref, lam_ref, nrm_ref = refs[:3]
    k_refs = refs[3:3 + n_seg]
    vt_refs = refs[3 + n_seg:3 + 2 * n_seg]
    o_ref = refs[3 + 2 * n_seg]
    lv = lam_ref[...]
    lam = (jnp.exp(jnp.sum(lv[0:1] * lv[1:2], axis=-1, keepdims=True))
           - jnp.exp(jnp.sum(lv[2:3] * lv[3:4], axis=-1, keepdims=True)) + lam_init)
    dv = 2 * DIFF_DH
    tq = q_ref.shape[1]
    subs = range(0, tq, Q_SUB)
    lane = lax.broadcasted_iota(jnp.int32, (Q_SUB, LANES), 1)
    zero = jnp.zeros((Q_SUB, LANES), BF16)
    chunks = _kv_chunks(k_refs, vt_refs, 0, 0, dv)
    chains = []
    for comp in range(2):
        keep = (lane < DIFF_DH) if comp == 0 else (lane >= DIFF_DH)
        chains += [(jnp.where(keep, q_ref[0, r:r + Q_SUB, :], zero), chunks) for r in subs]
    res = _flash_chains(chains)
    for ri, r in enumerate(subs):
        a0, a1 = res[ri], res[len(subs) + ri]
        o_t = a0[:dv] / a0[dv:dv + 1] - lam * (a1[:dv] / a1[dv:dv + 1])
        o_ref[0, r:r + Q_SUB, :] = (_rms(o_t.T, nrm_ref[...]) * (1.0 - lam_init)).astype(BF16)


def _diff(q, lam_w, nrm_w, ks, vts, *, lam_init):
    bsz, n, _ = q.shape
    tq = min(Q_TILE, n)
    n_seg = len(ks)
    return pl.pallas_call(
        functools.partial(_diff_kernel, n_seg=n_seg, lam_init=lam_init),
        grid=(bsz, DIFF_HEADS, n // tq),
        in_specs=[pl.BlockSpec((1, tq, LANES), lambda b, h, i: (b, i, h)),
                  _const_spec(lam_w.shape), _const_spec(nrm_w.shape)]
                 + [pl.BlockSpec((1, k.shape[1], LANES), lambda b, h, i: (b, 0, h)) for k in ks]
                 + [pl.BlockSpec((1, 2 * DIFF_DH, v.shape[2]), lambda b, h, i: (b, h, 0)) for v in vts],
        out_specs=pl.BlockSpec((1, tq, LANES), lambda b, h, i: (b, i, h)),
        out_shape=jax.ShapeDtypeStruct((bsz, n, DIFF_HEADS * 2 * DIFF_DH), BF16),
        compiler_params=_params(3),
        name=f'diff_attn_{n_seg}seg',
    )(q, lam_w, nrm_w, *ks, *vts)


def _swa_kernel(*refs, local, seq):
    if local:
        q_ref, sink_ref, kc_ref, vc_ref, kx_ref, vx_ref, o_ref = refs
    else:
        q_ref, sink_ref, kc_ref, vc_ref, o_ref = refs
    blk = WINDOW
    cols = SWA_GROUP * blk
    lane = lax.broadcasted_iota(jnp.int32, (blk, LANES), 1)
    kc = kc_ref[0]
    vct = vc_ref[0].T
    if local:
        i = pl.program_id(1)
        start = pl.multiple_of(jnp.clip((i - 1) * blk, 0, seq - 3 * blk), blk)
        kl = kx_ref[0, pl.ds(start, 3 * blk), :]
        vlt = vx_ref[0, pl.ds(start, 3 * blk), :].T
        kpos = start + lax.broadcasted_iota(jnp.int32, (3 * blk, cols), 0)
        qpos = i * blk + (lax.broadcasted_iota(jnp.int32, (3 * blk, cols), 1) & (blk - 1))
        visible = jnp.abs(qpos - kpos) <= WINDOW
    halves = []
    for half in range(SWA_KV_HEADS):
        keep = (lane < SWA_DH) if half == 0 else (lane >= SWA_DH)
        rows = slice(half * SWA_DH, (half + 1) * SWA_DH)
        q = jnp.concatenate(
            [jnp.where(keep, q_ref[0, :, c * LANES:(c + 1) * LANES], jnp.zeros((blk, LANES), BF16))
             for c in range(SWA_GROUP)], axis=0)
        sink = sink_ref[half] * LOG2E
        s_c = _dot_nt(kc, q)
        m = jnp.maximum(jnp.max(s_c, axis=0, keepdims=True), sink)
        if local:
            s_l = jnp.where(visible, _dot_nt(kl, q), -jnp.inf)
            m = jnp.maximum(m, jnp.max(s_l, axis=0, keepdims=True))
        p_c = jnp.exp2(s_c - m)
        l = jnp.sum(p_c, axis=0, keepdims=True) + jnp.exp2(sink - m)
        o = _dot(vct[rows], p_c.astype(BF16))
        if local:
            p_l = jnp.exp2(s_l - m)
            l = l + jnp.sum(p_l, axis=0, keepdims=True)
            o = o + _dot(vlt[rows], p_l.astype(BF16))
        halves.append(o / l)
    o_t = jnp.concatenate(halves, axis=0)
    for c in range(SWA_GROUP):
        o_ref[0, :, c * LANES:(c + 1) * LANES] = o_t[:, c * blk:(c + 1) * blk].T.astype(BF16)


def _swa(q, sink_cols, kc, vc, kx=None, vx=None):
    bsz, n, _ = q.shape
    local = kx is not None
    full = lambda a: pl.BlockSpec((1,) + a.shape[1:], lambda b, i: (b, 0, 0))
    kv = [kc, vc] + ([kx, vx] if local else [])
    return pl.pallas_call(
        functools.partial(_swa_kernel, local=local, seq=n),
        grid=(bsz, n // WINDOW),
        in_specs=[pl.BlockSpec((1, WINDOW, SWA_HEADS * SWA_DH), lambda b, i: (b, i, 0)),
                  _const_spec(sink_cols.shape)] + [full(a) for a in kv],
        out_specs=pl.BlockSpec((1, WINDOW, SWA_HEADS * SWA_DH), lambda b, i: (b, i, 0)),
        out_shape=jax.ShapeDtypeStruct((bsz, n, SWA_HEADS * SWA_DH), BF16),
        compiler_params=_params(2),
        name='swa_attn_local' if local else 'swa_attn_ctx',
    )(q, sink_cols, *kv)


def _merge_kernel(x_ref, mod_ref, post_ref, oa_ref, ob_ref, oc_ref, g_ref,
                  wa_ref, wb_ref, wc_ref, wo_ref, o_ref):
    d = D_MODEL
    y = (g_ref[0, :, 0:d].astype(F32) * _dot(oa_ref[0], wa_ref[...])
         + g_ref[0, :, d:2 * d].astype(F32) * _dot(ob_ref[0], wb_ref[...])
         + g_ref[0, :, 2 * d:3 * d].astype(F32) * _dot(oc_ref[0], wc_ref[...]))
    y = _dot(y.astype(BF16), wo_ref[...])
    mod = mod_ref[0]
    o_ref[0] = x_ref[0] + mod[2:3] * _rms(y, post_ref[...])


def _merge(x, mods, post_w, oa, ob, oc, gates, wa, wb, wc, wo):
    bsz, n, d = x.shape
    tm = min(ROW_TILE, n)
    per_batch_mod = mods.shape[0] > 1
    row_spec = lambda w: pl.BlockSpec((1, tm, w), lambda b, i: (b, i, 0))
    return pl.pallas_call(
        _merge_kernel,
        grid=(bsz, n // tm),
        in_specs=[row_spec(d),
                  pl.BlockSpec((1, N_MODS, d), (lambda b, i: (b, 0, 0)) if per_batch_mod else (lambda b, i: (0, 0, 0))),
                  _const_spec((1, d)),
                  row_spec(512), row_spec(512), row_spec(512), row_spec(3 * d),
                  _const_spec(wa.shape), _const_spec(wb.shape), _const_spec(wc.shape), _const_spec(wo.shape)],
        out_specs=row_spec(d),
        out_shape=jax.ShapeDtypeStruct((bsz, n, d), F32),
        compiler_params=_params(2),
        name='merge_out_proj',
    )(x, mods, post_w, oa, ob, oc, gates, wa, wb, wc, wo)


def _ffn_kernel(x_ref, mod_ref, pre_ref, post_ref, wgu_ref, wd_ref, o_ref):
    x = x_ref[0]
    mod = mod_ref[0]
    h = (_rms(x, pre_ref[...]) * (1.0 + mod[4:5]) + mod[3:4]).astype(BF16)
    g = _dot(h, wgu_ref[:, 0:D_FF])
    u = _dot(h, wgu_ref[:, D_FF:2 * D_FF])
    act = (g * jax.nn.sigmoid(g) * u).astype(BF16)
    f = _dot(act, wd_ref[...])
    o_ref[0] = x + mod[5:6] * _rms(f, post_ref[...])


def _ffn(x, mods, pre_w, post_w, wgu, wd):
    bsz, n, d = x.shape
    tm = min(ROW_TILE, n)
    per_batch_mod = mods.shape[0] > 1
    row_spec = pl.BlockSpec((1, tm, d), lambda b, i: (b, i, 0))
    return pl.pallas_call(
        _ffn_kernel,
        grid=(bsz, n // tm),
        in_specs=[row_spec,
                  pl.BlockSpec((1, N_MODS, d), (lambda b, i: (b, 0, 0)) if per_batch_mod else (lambda b, i: (0, 0, 0))),
                  _const_spec((1, d)), _const_spec((1, d)),
                  _const_spec(wgu.shape), _const_spec(wd.shape)],
        out_specs=row_spec,
        out_shape=jax.ShapeDtypeStruct((bsz, n, d), F32),
        compiler_params=_params(2),
        name='swiglu',
    )(x, mods, pre_w, post_w, wgu, wd)


def _rope_tables(seq):
    t = jnp.arange(seq)
    pos_row = (t // GRID_W).astype(F32)[:, None]
    pos_col = (t % GRID_W).astype(F32)[:, None]
    lanes = np.arange(LANES)

    def table(rot_dim, slot):
        axis_dim = rot_dim // 2
        half = axis_dim // 2
        inv = ROPE_BASE ** (-jnp.arange(0, axis_dim, 2, dtype=F32) / axis_dim)
        valid = slot >= 0
        axis, j = np.divmod(np.where(valid, slot, 0), axis_dim)
        ang = jnp.where(axis[None, :] == 0, pos_row, pos_col) * inv[j % half][None, :]
        sign = np.where(j < half, -1.0, 1.0).astype(np.float32)
        cos = jnp.where(valid[None, :], jnp.cos(ang), 1.0)
        sin = jnp.where(valid[None, :], jnp.sin(ang) * sign[None, :], 0.0)
        return cos, sin

    cos64, sin64 = table(SWA_DH, lanes % SWA_DH)
    in_rope = (lanes >= MLA_NOPE) & (lanes < MLA_NOPE + MLA_ROPE)
    cosm, sinm = table(MLA_ROPE, np.where(in_rope, lanes - MLA_NOPE, -1))
    return cos64, sin64, cosm, sinm


def _pack_in_proj(w):
    d = w.shape[0]
    ckv, kr, dk, dv, sk, sv, cq, dq, sq, gates = jnp.split(
        w, np.cumsum([256, 32, 512, 512, 128, 128, 256, 512, 512]).tolist(), axis=1)
    kr_wide = jnp.zeros((d, LANES), w.dtype).at[:, MLA_NOPE:MLA_NOPE + MLA_ROPE].set(kr)
    sq = sq.reshape(d, SWA_HEADS, SWA_DH)[:, np.array(SWA_HEAD_PERM)].reshape(d, SWA_HEADS * SWA_DH)
    parts = dict(ckv=ckv, cq=cq, kr=kr_wide, dk=dk, sk=sk, dq=dq, sq=sq, sv=sv, gates=gates)
    return jnp.concatenate([parts[name] for name in _COLS], axis=1).astype(BF16), dv.T.astype(BF16)


def _pad_heads(w, used):
    r = w.shape[0]
    w = w.reshape(r, MLA_HEADS, used)
    return jnp.pad(w, ((0, 0), (0, 0), (0, LANES - used))).reshape(r, MLA_HEADS * LANES)


def kernel(x, c, ctx, c_ctx, w_ada, b_ada, attn_pre_norm, attn_post_norm, ffn_pre_norm, ffn_post_norm, w_in, mla_q_norm, w_uq, mla_kv_norm, w_ukv, diff_lambda, diff_norm, swa_sink, w_branch_a, w_branch_b, w_branch_c, w_o, w_gate_up, w_down):
    bsz, seq, d = x.shape
    depth = w_ada.shape[0]
    tabs = _rope_tables(seq)
    cond = jnp.concatenate([c, c_ctx[None, :], jnp.zeros((8 - bsz - 1, d), c.dtype)], axis=0)
    row = lambda v: v.reshape(1, -1)

    for l in range(depth):
        last = l == depth - 1
        lam_init = 0.8 - 0.6 * math.exp(-0.3 * l)
        mods = _adaln(cond, w_ada[l], b_ada[l])
        mx = mods[:bsz].reshape(bsz, N_MODS, d)
        mc = mods[bsz:bsz + 1].reshape(1, N_MODS, d)

        wcat, wdvt = _pack_in_proj(w_in[l])
        ukv = w_ukv[l].reshape(MLA_RANK, MLA_HEADS, MLA_NOPE + MLA_V)
        wuk = _pad_heads(ukv[:, :, :MLA_NOPE].reshape(MLA_RANK, -1), MLA_NOPE).astype(BF16)
        wuvt = ukv[:, :, MLA_NOPE:].reshape(MLA_RANK, MLA_HEADS * MLA_V).T.astype(BF16)
        wuq = _pad_heads(w_uq[l], MLA_NOPE + MLA_ROPE).astype(BF16)
        proj_w = (wcat, wdvt, row(mla_kv_norm[l]), wuk, wuvt, row(mla_q_norm[l]), wuq)
        sink_cols = jnp.repeat(swa_sink[l].reshape(SWA_KV_HEADS, 1, SWA_GROUP), WINDOW, axis=2)
        wa, wb = w_branch_a[l].astype(BF16), w_branch_b[l].astype(BF16)
        wc = w_branch_c[l].reshape(SWA_HEADS, SWA_DH, d)[np.array(SWA_HEAD_PERM)].reshape(-1, d).astype(BF16)
        wo, wgu, wd = w_o[l].astype(BF16), w_gate_up[l].astype(BF16), w_down[l].astype(BF16)
        dnorm = row(diff_norm[l])
        pre_a, post_a = row(attn_pre_norm[l]), row(attn_post_norm[l])
        pre_f, post_f = row(ffn_pre_norm[l]), row(ffn_post_norm[l])

        px = _in_proj(x, mx, pre_a, tabs, *proj_w, rope=True, with_q=True)
        pc = _in_proj(ctx, mc, pre_a, tabs, *proj_w, rope=False, with_q=not last)
        mkx, mvx, dkx, dvx, skx, svx, mqx, dqx, sqx, gx = px
        mkc, mvc, dkc, dvc, skc, svc = pc[:6]
        o_a = _mla(mqx, [mkc, mkx], [mvc, mvx])
        o_b = _diff(dqx, diff_lambda[l], dnorm, [dkc, dkx], [dvc, dvx], lam_init=lam_init)
        o_c = _swa(sqx, sink_cols, skc, svc, skx, svx)
        x = _merge(x, mx, post_a, o_a, o_b, o_c, gx, wa, wb, wc, wo)
        if not last:
            mqc, dqc, sqc, gc = pc[6:]
            p_a = _mla(mqc, [mkc], [mvc])
            p_b = _diff(dqc, diff_lambda[l], dnorm, [dkc], [dvc], lam_init=lam_init)
            p_c = _swa(sqc, sink_cols, skc, svc)
            ctx = _merge(ctx, mc, post_a, p_a, p_b, p_c, gc, wa, wb, wc, wo)

        x = _ffn(x, mx, pre_f, post_f, wgu, wd)
        if not last:
            ctx = _ffn(ctx, mc, pre_f, post_f, wgu, wd)
    return x
```

```python
import functools
import math

import jax
import jax.numpy as jnp
import numpy as np
from jax import lax
from jax.experimental import pallas as pl
from jax.experimental.pallas import tpu as pltpu

D_MODEL = 1024
GRID_W = 64
EPS = 1e-6
ROPE_BASE = 10000.0
LOG2E = math.log2(math.e)

MLA_HEADS = 8
MLA_RANK = 256
MLA_NOPE = 64
MLA_ROPE = 32
MLA_V = 64
MLA_SCALE = (MLA_NOPE + MLA_ROPE) ** -0.5

DIFF_HEADS = 4
DIFF_DH = 64
DIFF_SCALE = DIFF_DH ** -0.5

SWA_HEADS = 8
SWA_KV_HEADS = 2
SWA_DH = 64
SWA_GROUP = SWA_HEADS // SWA_KV_HEADS
WINDOW = 128
SWA_SCALE = SWA_DH ** -0.5

D_FF = 2816
N_MODS = 6

LANES = 128
BF16_ROWS = 16
VMEM_LIMIT = 56 * 1024 * 1024

ROW_TILE = 512
Q_TILE = 512
Q_SUB = 256
KEY_CHUNK = 1024
SWA_BLOCKS = 4
LOGIT_BOUND = 64.0
VALUE_BOUND = 2.0 ** 40

_COLS = {}
_off = 0
for _name, _width in (('ckv', 256), ('cq', 256), ('kr', 128), ('dk', 512), ('sk', 128),
                      ('dq', 512), ('sq', 512), ('sv', 128), ('gates', 3072)):
    _COLS[_name] = (_off, _off + _width)
    _off += _width
IN_COLS = _off
SWA_HEAD_PERM = (0, 4, 1, 5, 2, 6, 3, 7)

BF16 = jnp.bfloat16
F32 = jnp.float32


def _dot(a, b):
    return jnp.dot(a, b, preferred_element_type=F32)


def _dot_nt(a, b):
    return lax.dot_general(a, b, (((1,), (1,)), ((), ())), preferred_element_type=F32)


def _rms(x, w):
    return x * lax.rsqrt(jnp.mean(x * x, axis=-1, keepdims=True) + EPS) * w


def _rope(x, cos, sin, first, shift):
    partner = jnp.where(first, pltpu.roll(x, LANES - shift, 1), pltpu.roll(x, shift, 1))
    return x * cos + partner * sin


def _const_spec(shape):
    nd = len(shape)
    return pl.BlockSpec(shape, lambda *_: (0,) * nd, pipeline_mode=pl.Buffered(1))


def _params(n_axes):
    return pltpu.CompilerParams(dimension_semantics=('arbitrary',) * n_axes,
                                vmem_limit_bytes=VMEM_LIMIT)


def _adaln_kernel(c_ref, w_ref, b_ref, o_ref):
    c = c_ref[...]
    s = c * jax.nn.sigmoid(c)
    o_ref[...] = _dot(s.astype(BF16), w_ref[...].astype(BF16)) + b_ref[...]


def _adaln(cond, w, b):
    rows, d = cond.shape
    n = w.shape[1]
    tn = n // 4
    return pl.pallas_call(
        _adaln_kernel,
        grid=(n // tn,),
        in_specs=[pl.BlockSpec((rows, d), lambda j: (0, 0)),
                  pl.BlockSpec((d, tn), lambda j: (0, j)),
                  pl.BlockSpec((1, tn), lambda j: (0, j))],
        out_specs=pl.BlockSpec((rows, tn), lambda j: (0, j)),
        out_shape=jax.ShapeDtypeStruct((rows, n), F32),
        compiler_params=_params(1),
        name='adaln',
    )(cond, w, b.reshape(1, n))


def _in_proj_kernel(x_ref, mod_ref, pre_ref, cos64_ref, sin64_ref, cosm_ref, sinm_ref,
                    w_ref, wdvt_ref, kvn_ref, wuk_ref, wuvt_ref, qn_ref, wuq_ref, *outs, rope, with_q):
    x = x_ref[0]
    mod = mod_ref[0]
    h = _rms(x, pre_ref[...]) * (1.0 + mod[1:2]) + mod[0:1]
    hb = h.astype(BF16)

    def proj(name):
        a, b = _COLS[name]
        return _dot(hb, w_ref[:, a:b])

    lane = lax.broadcasted_iota(jnp.int32, (x.shape[0], LANES), 1)
    first64 = (lane & 31) < 16
    firstm = (lane & 15) < 8
    if rope:
        cos64, sin64, cosm, sinm = cos64_ref[...], sin64_ref[...], cosm_ref[...], sinm_ref[...]

    def rope64(z, scale):
        if not rope:
            return z if scale == 1.0 else z * scale
        c, s = (cos64, sin64) if scale == 1.0 else (cos64 * scale, sin64 * scale)
        return jnp.concatenate(
            [_rope(z[:, i:i + LANES], c, s, first64, 16) for i in range(0, z.shape[1], LANES)], axis=1)

    (mk_ref, mvt_ref, dk_ref, dvt_ref, sk_ref, sv_ref), q_outs = outs[:6], outs[6:]

    ckv = _rms(proj('ckv'), kvn_ref[...]).astype(BF16)
    k_nope = _dot(ckv, wuk_ref[...])
    kr = proj('kr')
    if rope:
        kr = _rope(kr, cosm, sinm, firstm, 8)
    for i in range(0, MLA_HEADS * LANES, LANES):
        mk_ref[0, :, i:i + LANES] = (k_nope[:, i:i + LANES] + kr).astype(BF16)
    mvt_ref[0] = _dot_nt(wuvt_ref[...], ckv).astype(BF16)

    dk_ref[0] = rope64(proj('dk'), 1.0).astype(BF16)
    dvt_ref[0] = _dot_nt(wdvt_ref[...], hb).astype(BF16)
    sk_ref[0] = rope64(proj('sk'), 1.0).astype(BF16)
    sv_ref[0] = proj('sv').astype(BF16)

    if with_q:
        mq_ref, dq_ref, sq_ref, g_ref = q_outs
        cq = _rms(proj('cq'), qn_ref[...]).astype(BF16)
        q = _dot(cq, wuq_ref[...])
        mla_scale = MLA_SCALE * LOG2E
        for i in range(0, MLA_HEADS * LANES, LANES):
            qi = q[:, i:i + LANES]
            if rope:
                qi = _rope(qi, cosm * mla_scale, sinm * mla_scale, firstm, 8)
            else:
                qi = qi * mla_scale
            mq_ref[0, :, i:i + LANES] = qi.astype(BF16)
        dq_ref[0] = rope64(proj('dq'), DIFF_SCALE * LOG2E).astype(BF16)
        sq_ref[0] = rope64(proj('sq'), SWA_SCALE * LOG2E).astype(BF16)
        a, b = _COLS['gates']
        for i in range(a, b, 512):
            g_ref[0, :, i - a:i - a + 512] = jax.nn.sigmoid(_dot(hb, w_ref[:, i:i + 512])).astype(BF16)


def _in_proj(x, mods, pre_w, tabs, wcat, wdvt, kvn, wuk, wuvt, qn, wuq, *, rope, with_q):
    bsz, n, d = x.shape
    tm = min(ROW_TILE, n)
    per_batch_mod = mods.shape[0] > 1
    outs = [(MLA_HEADS * LANES, False), (MLA_HEADS * MLA_V, True), (512, False), (512, True),
            (128, False), (128, False)]
    if with_q:
        outs += [(MLA_HEADS * LANES, False), (512, False), (512, False), (3 * D_MODEL, False)]
    row_spec = lambda w: pl.BlockSpec((1, tm, w), lambda b, i: (b, i, 0))
    col_spec = lambda w: pl.BlockSpec((1, w, tm), lambda b, i: (b, 0, i))
    tab_spec = pl.BlockSpec((tm, LANES), lambda b, i: (i, 0))
    kern = functools.partial(_in_proj_kernel, rope=rope, with_q=with_q)
    return pl.pallas_call(
        kern,
        grid=(bsz, n // tm),
        in_specs=[row_spec(d),
                  pl.BlockSpec((1, N_MODS, d), (lambda b, i: (b, 0, 0)) if per_batch_mod else (lambda b, i: (0, 0, 0))),
                  _const_spec((1, d)),
                  tab_spec, tab_spec, tab_spec, tab_spec,
                  _const_spec(wcat.shape), _const_spec(wdvt.shape), _const_spec(kvn.shape),
                  _const_spec(wuk.shape), _const_spec(wuvt.shape), _const_spec(qn.shape), _const_spec(wuq.shape)],
        out_specs=[col_spec(w) if t else row_spec(w) for w, t in outs],
        out_shape=[jax.ShapeDtypeStruct((bsz, w, n) if t else (bsz, n, w), BF16) for w, t in outs],
        compiler_params=_params(2),
        name='in_proj_rope' if rope else 'in_proj_ctx',
    )(x, mods, pre_w, *tabs, wcat, wdvt, kvn, wuk, wuvt, qn, wuq)


def _kv_chunks(k_refs, vt_refs, col, vrow, dv):
    chunks = []
    for k_ref, vt_ref in zip(k_refs, vt_refs):
        n = k_ref.shape[1]
        step = min(KEY_CHUNK, n)
        for c in range(0, n, step):
            chunks.append((functools.partial(lambda r, c, s: r[0, c:c + s, col:col + LANES], k_ref, c, step),
                           functools.partial(lambda r, c, s: r[0, vrow:vrow + dv, c:c + s], vt_ref, c, step)))
    return chunks


def _flash_chains(chains):
    n = len(chains[0][1])
    state = [(None, None)] * len(chains)
    s_next = [_dot_nt(chunks[0][0](), q) for q, chunks in chains]
    for i in range(n):
        for ci, (q, chunks) in enumerate(chains):
            m, acc = state[ci]
            s = s_next[ci]
            if i + 1 < n:
                s_next[ci] = _dot_nt(chunks[i + 1][0](), q)
            mc = jnp.max(s, axis=0, keepdims=True)
            m_new = mc if m is None else jnp.maximum(m, mc)
            pv = _dot(_with_ones_rows(chunks[i][1]()), jnp.exp2(s - m_new).astype(BF16))
            acc = pv if m is None else jnp.exp2(m - m_new) * acc + pv
            state[ci] = (m_new, acc)
    return [acc for _, acc in state]


def _unshifted_chains(chains):
    n = len(chains[0][1])
    acc = [None] * len(chains)
    s_next = [_dot_nt(chunks[0][0](), q) for q, chunks in chains]
    for i in range(n):
        for ci, (q, chunks) in enumerate(chains):
            s = s_next[ci]
            if i + 1 < n:
                s_next[ci] = _dot_nt(chunks[i + 1][0](), q)
            pv = _dot(_with_ones_rows(chunks[i][1]()), jnp.exp2(s).astype(BF16))
            acc[ci] = pv if acc[ci] is None else acc[ci] + pv
    return acc


def _with_ones_rows(vt):
    return jnp.concatenate([vt, jnp.ones((BF16_ROWS, vt.shape[1]), BF16)], axis=0)


def _row_norm2_max(x):
    x = x.astype(F32)
    return jnp.max(jnp.sum(x * x, axis=1, keepdims=True), axis=0, keepdims=True)


def _store_key_bound(bound_ref, k_refs, vt_refs, n_heads):
    kn = functools.reduce(jnp.maximum, [_row_norm2_max(k_ref[0, :, j * LANES:(j + 1) * LANES])
                                        for k_ref in k_refs for j in range(n_heads)])
    vmax = functools.reduce(jnp.maximum, [
        jnp.max(jnp.max(jnp.abs(vt_ref[0].astype(F32)), axis=1, keepdims=True), axis=0, keepdims=True)
        for vt_ref in vt_refs])
    bound_ref[...] = jnp.broadcast_to(jnp.where(vmax <= VALUE_BOUND, kn, jnp.inf), bound_ref.shape)


def _logits_bounded(q_ref, bound_ref, n_heads):
    qn = functools.reduce(jnp.maximum, [_row_norm2_max(q_ref[0, :, j * LANES:(j + 1) * LANES])
                                        for j in range(n_heads)])
    return jnp.max(qn * bound_ref[0:1, 0:1]) <= LOGIT_BOUND * LOGIT_BOUND


def _run_chains(chains, bounded, finish):
    @pl.when(bounded)
    def _():
        finish(_unshifted_chains(chains))

    @pl.when(jnp.logical_not(bounded))
    def _():
        finish(_flash_chains(chains))


def _mla_kernel(*refs, n_seg):
    q_ref = refs[0]
    k_refs = refs[1:1 + n_seg]
    vt_refs = refs[1 + n_seg:1 + 2 * n_seg]
    o_ref, bound_ref = refs[1 + 2 * n_seg:]

    @pl.when(pl.program_id(2) == 0)
    def _():
        _store_key_bound(bound_ref, k_refs, vt_refs, 2)

    tq = q_ref.shape[1]
    subs = range(0, tq, Q_SUB)
    chains = []
    for j in range(2):
        chunks = _kv_chunks(k_refs, vt_refs, j * LANES, j * MLA_V, MLA_V)
        chains += [(q_ref[0, r:r + Q_SUB, j * LANES:(j + 1) * LANES], chunks) for r in subs]

    def finish(res):
        for ri, r in enumerate(subs):
            pair = [res[j * len(subs) + ri] for j in range(2)]
            o_t = jnp.concatenate([a[:MLA_V] / a[MLA_V:MLA_V + 1] for a in pair], axis=0)
            o_ref[0, r:r + Q_SUB, :] = o_t.T.astype(BF16)

    _run_chains(chains, _logits_bounded(q_ref, bound_ref, 2), finish)


def _mla(q, ks, vts):
    bsz, n, _ = q.shape
    tq = min(Q_TILE, n)
    n_seg = len(ks)
    return pl.pallas_call(
        functools.partial(_mla_kernel, n_seg=n_seg),
        grid=(bsz, MLA_HEADS // 2, n // tq),
        in_specs=[pl.BlockSpec((1, tq, 2 * LANES), lambda b, h, i: (b, i, h))]
                 + [pl.BlockSpec((1, k.shape[1], 2 * LANES), lambda b, h, i: (b, 0, h)) for k in ks]
                 + [pl.BlockSpec((1, 2 * MLA_V, v.shape[2]), lambda b, h, i: (b, h, 0)) for v in vts],
        out_specs=pl.BlockSpec((1, tq, LANES), lambda b, h, i: (b, i, h)),
        out_shape=jax.ShapeDtypeStruct((bsz, n, MLA_HEADS * MLA_V), BF16),
        scratch_shapes=[pltpu.VMEM((8, LANES), F32)],
        compiler_params=_params(3),
        name=f'mla_attn_{n_seg}seg',
    )(q, *ks, *vts)


def _diff_kernel(*refs, n_seg, lam_init):
    q_ref, lam_ref, nrm_ref = refs[:3]
    k_refs = refs[3:3 + n_seg]
    vt_refs = refs[3 + n_seg:3 + 2 * n_seg]
    o_ref, bound_ref = refs[3 + 2 * n_seg:]

    @pl.when(pl.program_id(2) == 0)
    def _():
        _store_key_bound(bound_ref, k_refs, vt_refs, 1)

    lv = lam_ref[...]
    lam = (jnp.exp(jnp.sum(lv[0:1] * lv[1:2], axis=-1, keepdims=True))
           - jnp.exp(jnp.sum(lv[2:3] * lv[3:4], axis=-1, keepdims=True)) + lam_init)
    dv = 2 * DIFF_DH
    tq = q_ref.shape[1]
    subs = range(0, tq, Q_SUB)
    lane = lax.broadcasted_iota(jnp.int32, (Q_SUB, LANES), 1)
    zero = jnp.zeros((Q_SUB, LANES), BF16)
    chunks = _kv_chunks(k_refs, vt_refs, 0, 0, dv)
    chains = []
    for comp in range(2):
        keep = (lane < DIFF_DH) if comp == 0 else (lane >= DIFF_DH)
        chains += [(jnp.where(keep, q_ref[0, r:r + Q_SUB, :], zero), chunks) for r in subs]

    def finish(res):
        for ri, r in enumerate(subs):
            a0, a1 = res[ri], res[len(subs) + ri]
            o_t = a0[:dv] / a0[dv:dv + 1] - lam * (a1[:dv] / a1[dv:dv + 1])
            o_ref[0, r:r + Q_SUB, :] = (_rms(o_t.T, nrm_ref[...]) * (1.0 - lam_init)).astype(BF16)

    _run_chains(chains, _logits_bounded(q_ref, bound_ref, 1), finish)


def _diff(q, lam_w, nrm_w, ks, vts, *, lam_init):
    bsz, n, _ = q.shape
    tq = min(Q_TILE, n)
    n_seg = len(ks)
    return pl.pallas_call(
        functools.partial(_diff_kernel, n_seg=n_seg, lam_init=lam_init),
        grid=(bsz, DIFF_HEADS, n // tq),
        in_specs=[pl.BlockSpec((1, tq, LANES), lambda b, h, i: (b, i, h)),
                  _const_spec(lam_w.shape), _const_spec(nrm_w.shape)]
                 + [pl.BlockSpec((1, k.shape[1], LANES), lambda b, h, i: (b, 0, h)) for k in ks]
                 + [pl.BlockSpec((1, 2 * DIFF_DH, v.shape[2]), lambda b, h, i: (b, h, 0)) for v in vts],
        out_specs=pl.BlockSpec((1, tq, LANES), lambda b, h, i: (b, i, h)),
        out_shape=jax.ShapeDtypeStruct((bsz, n, DIFF_HEADS * 2 * DIFF_DH), BF16),
        scratch_shapes=[pltpu.VMEM((8, LANES), F32)],
        compiler_params=_params(3),
        name=f'diff_attn_{n_seg}seg',
    )(q, lam_w, nrm_w, *ks, *vts)


def _swa_kernel(*refs, local, seq):
    bound_ref = refs[-1]
    refs = refs[:-1]
    if local:
        q_ref, sink_ref, kc_ref, vc_ref, kx_ref, vx_ref, o_ref = refs
    else:
        q_ref, sink_ref, kc_ref, vc_ref, o_ref = refs
    blk = WINDOW
    n_blocks = q_ref.shape[1] // blk
    nb = seq // blk
    cols = SWA_GROUP * blk
    g = pl.program_id(1)

    @pl.when(g == 0)
    def _():
        kv = [(kc_ref, vc_ref)] + ([(kx_ref, vx_ref)] if local else [])
        kn = functools.reduce(jnp.maximum, [_row_norm2_max(k_ref[0]) for k_ref, _ in kv])
        vmax = functools.reduce(jnp.maximum, [
            jnp.max(jnp.max(jnp.abs(v_ref[0].astype(F32)), axis=1, keepdims=True), axis=0, keepdims=True)
            for _, v_ref in kv])
        bound_ref[...] = jnp.broadcast_to(jnp.where(vmax <= VALUE_BOUND, kn, jnp.inf), bound_ref.shape)

    sinks = [sink_ref[half] * LOG2E for half in range(SWA_KV_HEADS)]
    sink_max = functools.reduce(jnp.maximum, [jnp.max(jnp.abs(s), axis=1, keepdims=True) for s in sinks])
    bounded = jnp.logical_and(_logits_bounded(q_ref, bound_ref, SWA_GROUP), jnp.max(sink_max) <= LOGIT_BOUND)

    lane = lax.broadcasted_iota(jnp.int32, (blk, LANES), 1)
    kc = kc_ref[0]
    vct = vc_ref[0].T
    if local:
        key_r = lax.broadcasted_iota(jnp.int32, (blk, cols), 0)
        qry_r = lax.broadcasted_iota(jnp.int32, (blk, cols), 1) & (blk - 1)

        def block_rows(ref, idx):
            return ref[0, pl.ds(pl.multiple_of(idx * blk, blk), blk), :]

        first = g * n_blocks
        win = [jnp.clip(first - 1 + t, 0, nb - 1) for t in range(n_blocks + 2)]
        k_blocks = [block_rows(kx_ref, w) for w in win]
        vt_blocks = [block_rows(vx_ref, w).T for w in win]

    def run(shifted):
        for b in range(n_blocks):
            if local:
                i = first + b
                prev_vis = jnp.logical_and(qry_r <= key_r, i > 0)
                next_vis = jnp.logical_and(key_r <= qry_r, i < nb - 1)
                kl = jnp.concatenate(k_blocks[b:b + 3], axis=0)
                vlt = jnp.concatenate(vt_blocks[b:b + 3], axis=1)
            halves = []
            for half in range(SWA_KV_HEADS):
                keep = (lane < SWA_DH) if half == 0 else (lane >= SWA_DH)
                rows = slice(half * SWA_DH, (half + 1) * SWA_DH)
                q = jnp.concatenate(
                    [jnp.where(keep, q_ref[0, b * blk:(b + 1) * blk, c * LANES:(c + 1) * LANES],
                               jnp.zeros((blk, LANES), BF16)) for c in range(SWA_GROUP)], axis=0)
                sink = sinks[half]
                s_c = _dot_nt(kc, q)
                if local:
                    s_l = _dot_nt(kl, q)
                if shifted:
                    m = jnp.maximum(jnp.max(s_c, axis=0, keepdims=True), sink)
                    if local:
                        neg = -jnp.inf
                        m = jnp.maximum(m, jnp.max(jnp.where(prev_vis, s_l[:blk], neg), axis=0, keepdims=True))
                        m = jnp.maximum(m, jnp.max(s_l[blk:2 * blk], axis=0, keepdims=True))
                        m = jnp.maximum(m, jnp.max(jnp.where(next_vis, s_l[2 * blk:], neg), axis=0, keepdims=True))
                        s_l = s_l - m
                    s_c, sink = s_c - m, sink - m
                acc = _dot(_with_ones_rows(vct[rows]), jnp.exp2(s_c).astype(BF16))
                if local:
                    e_l = jnp.exp2(s_l)
                    p_l = jnp.concatenate([jnp.where(prev_vis, e_l[:blk], 0.0), e_l[blk:2 * blk],
                                           jnp.where(next_vis, e_l[2 * blk:], 0.0)], axis=0)
                    acc = acc + _dot(_with_ones_rows(vlt[rows]), p_l.astype(BF16))
                halves.append(acc[:SWA_DH] / (acc[SWA_DH:SWA_DH + 1] + jnp.exp2(sink)))
            o_t = jnp.concatenate(halves, axis=0)
            for c in range(SWA_GROUP):
                o_ref[0, b * blk:(b + 1) * blk, c * LANES:(c + 1) * LANES] = \
                    o_t[:, c * blk:(c + 1) * blk].T.astype(BF16)

    @pl.when(bounded)
    def _():
        run(False)

    @pl.when(jnp.logical_not(bounded))
    def _():
        run(True)


def _swa(q, sink_cols, kc, vc, kx=None, vx=None):
    bsz, n, _ = q.shape
    local = kx is not None
    rows = min(SWA_BLOCKS * WINDOW, n)
    full = lambda a: pl.BlockSpec((1,) + a.shape[1:], lambda b, i: (b, 0, 0))
    kv = [kc, vc] + ([kx, vx] if local else [])
    return pl.pallas_call(
        functools.partial(_swa_kernel, local=local, seq=n),
        grid=(bsz, n // rows),
        in_specs=[pl.BlockSpec((1, rows, SWA_HEADS * SWA_DH), lambda b, i: (b, i, 0)),
                  _const_spec(sink_cols.shape)] + [full(a) for a in kv],
        out_specs=pl.BlockSpec((1, rows, SWA_HEADS * SWA_DH), lambda b, i: (b, i, 0)),
        out_shape=jax.ShapeDtypeStruct((bsz, n, SWA_HEADS * SWA_DH), BF16),
        scratch_shapes=[pltpu.VMEM((8, LANES), F32)],
        compiler_params=_params(2),
        name='swa_attn_local' if local else 'swa_attn_ctx',
    )(q, sink_cols, *kv)


def _merge_kernel(x_ref, mod_ref, post_ref, oa_ref, ob_ref, oc_ref, g_ref,
                  wa_ref, wb_ref, wc_ref, wo_ref, o_ref):
    d = D_MODEL
    y = (g_ref[0, :, 0:d].astype(F32) * _dot(oa_ref[0], wa_ref[...])
         + g_ref[0, :, d:2 * d].astype(F32) * _dot(ob_ref[0], wb_ref[...])
         + g_ref[0, :, 2 * d:3 * d].astype(F32) * _dot(oc_ref[0], wc_ref[...]))
    y = _dot(y.astype(BF16), wo_ref[...])
    mod = mod_ref[0]
    o_ref[0] = x_ref[0] + mod[2:3] * _rms(y, post_ref[...])


def _merge(x, mods, post_w, oa, ob, oc, gates, wa, wb, wc, wo):
    bsz, n, d = x.shape
    tm = min(ROW_TILE, n)
    per_batch_mod = mods.shape[0] > 1
    row_spec = lambda w: pl.BlockSpec((1, tm, w), lambda b, i: (b, i, 0))
    return pl.pallas_call(
        _merge_kernel,
        grid=(bsz, n // tm),
        in_specs=[row_spec(d),
                  pl.BlockSpec((1, N_MODS, d), (lambda b, i: (b, 0, 0)) if per_batch_mod else (lambda b, i: (0, 0, 0))),
                  _const_spec((1, d)),
                  row_spec(512), row_spec(512), row_spec(512), row_spec(3 * d),
                  _const_spec(wa.shape), _const_spec(wb.shape), _const_spec(wc.shape), _const_spec(wo.shape)],
        out_specs=row_spec(d),
        out_shape=jax.ShapeDtypeStruct((bsz, n, d), F32),
        compiler_params=_params(2),
        name='merge_out_proj',
    )(x, mods, post_w, oa, ob, oc, gates, wa, wb, wc, wo)


def _ffn_kernel(x_ref, mod_ref, pre_ref, post_ref, wgu_ref, wd_ref, o_ref):
    x = x_ref[0]
    mod = mod_ref[0]
    h = (_rms(x, pre_ref[...]) * (1.0 + mod[4:5]) + mod[3:4]).astype(BF16)
    g = _dot(h, wgu_ref[:, 0:D_FF])
    u = _dot(h, wgu_ref[:, D_FF:2 * D_FF])
    act = (g * jax.nn.sigmoid(g) * u).astype(BF16)
    f = _dot(act, wd_ref[...])
    o_ref[0] = x + mod[5:6] * _rms(f, post_ref[...])


def _ffn(x, mods, pre_w, post_w, wgu, wd):
    bsz, n, d = x.shape
    tm = min(ROW_TILE, n)
    per_batch_mod = mods.shape[0] > 1
    row_spec = pl.BlockSpec((1, tm, d), lambda b, i: (b, i, 0))
    return pl.pallas_call(
        _ffn_kernel,
        grid=(bsz, n // tm),
        in_specs=[row_spec,
                  pl.BlockSpec((1, N_MODS, d), (lambda b, i: (b, 0, 0)) if per_batch_mod else (lambda b, i: (0, 0, 0))),
                  _const_spec((1, d)), _const_spec((1, d)),
                  _const_spec(wgu.shape), _const_spec(wd.shape)],
        out_specs=row_spec,
        out_shape=jax.ShapeDtypeStruct((bsz, n, d), F32),
        compiler_params=_params(2),
        name='swiglu',
    )(x, mods, pre_w, post_w, wgu, wd)


def _rope_tables(seq):
    t = jnp.arange(seq)
    pos_row = (t // GRID_W).astype(F32)[:, None]
    pos_col = (t % GRID_W).astype(F32)[:, None]
    lanes = np.arange(LANES)

    def table(rot_dim, slot):
        axis_dim = rot_dim // 2
        half = axis_dim // 2
        inv = ROPE_BASE ** (-jnp.arange(0, axis_dim, 2, dtype=F32) / axis_dim)
        valid = slot >= 0
        axis, j = np.divmod(np.where(valid, slot, 0), axis_dim)
        ang = jnp.where(axis[None, :] == 0, pos_row, pos_col) * inv[j % half][None, :]
        sign = np.where(j < half, -1.0, 1.0).astype(np.float32)
        cos = jnp.where(valid[None, :], jnp.cos(ang), 1.0)
        sin = jnp.where(valid[None, :], jnp.sin(ang) * sign[None, :], 0.0)
        return cos, sin

    cos64, sin64 = table(SWA_DH, lanes % SWA_DH)
    in_rope = (lanes >= MLA_NOPE) & (lanes < MLA_NOPE + MLA_ROPE)
    cosm, sinm = table(MLA_ROPE, np.where(in_rope, lanes - MLA_NOPE, -1))
    return cos64, sin64, cosm, sinm


def _pack_in_proj(w):
    d = w.shape[0]
    ckv, kr, dk, dv, sk, sv, cq, dq, sq, gates = jnp.split(
        w, np.cumsum([256, 32, 512, 512, 128, 128, 256, 512, 512]).tolist(), axis=1)
    kr_wide = jnp.zeros((d, LANES), w.dtype).at[:, MLA_NOPE:MLA_NOPE + MLA_ROPE].set(kr)
    sq = sq.reshape(d, SWA_HEADS, SWA_DH)[:, np.array(SWA_HEAD_PERM)].reshape(d, SWA_HEADS * SWA_DH)
    parts = dict(ckv=ckv, cq=cq, kr=kr_wide, dk=dk, sk=sk, dq=dq, sq=sq, sv=sv, gates=gates)
    return jnp.concatenate([parts[name] for name in _COLS], axis=1).astype(BF16), dv.T.astype(BF16)


def _pad_heads(w, used):
    r = w.shape[0]
    w = w.reshape(r, MLA_HEADS, used)
    return jnp.pad(w, ((0, 0), (0, 0), (0, LANES - used))).reshape(r, MLA_HEADS * LANES)


def kernel(x, c, ctx, c_ctx, w_ada, b_ada, attn_pre_norm, attn_post_norm, ffn_pre_norm, ffn_post_norm, w_in, mla_q_norm, w_uq, mla_kv_norm, w_ukv, diff_lambda, diff_norm, swa_sink, w_branch_a, w_branch_b, w_branch_c, w_o, w_gate_up, w_down):
    bsz, seq, d = x.shape
    depth = w_ada.shape[0]
    tabs = _rope_tables(seq)
    cond = jnp.concatenate([c, c_ctx[None, :], jnp.zeros((8 - bsz - 1, d), c.dtype)], axis=0)
    row = lambda v: v.reshape(1, -1)

    for l in range(depth):
        last = l == depth - 1
        lam_init = 0.8 - 0.6 * math.exp(-0.3 * l)
        mods = _adaln(cond, w_ada[l], b_ada[l])
        mx = mods[:bsz].reshape(bsz, N_MODS, d)
        mc = mods[bsz:bsz + 1].reshape(1, N_MODS, d)

        wcat, wdvt = _pack_in_proj(w_in[l])
        ukv = w_ukv[l].reshape(MLA_RANK, MLA_HEADS, MLA_NOPE + MLA_V)
        wuk = _pad_heads(ukv[:, :, :MLA_NOPE].reshape(MLA_RANK, -1), MLA_NOPE).astype(BF16)
        wuvt = ukv[:, :, MLA_NOPE:].reshape(MLA_RANK, MLA_HEADS * MLA_V).T.astype(BF16)
        wuq = _pad_heads(w_uq[l], MLA_NOPE + MLA_ROPE).astype(BF16)
        proj_w = (wcat, wdvt, row(mla_kv_norm[l]), wuk, wuvt, row(mla_q_norm[l]), wuq)
        sink_cols = jnp.repeat(swa_sink[l].reshape(SWA_KV_HEADS, 1, SWA_GROUP), WINDOW, axis=2)
        wa, wb = w_branch_a[l].astype(BF16), w_branch_b[l].astype(BF16)
        wc = w_branch_c[l].reshape(SWA_HEADS, SWA_DH, d)[np.array(SWA_HEAD_PERM)].reshape(-1, d).astype(BF16)
        wo, wgu, wd = w_o[l].astype(BF16), w_gate_up[l].astype(BF16), w_down[l].astype(BF16)
        dnorm = row(diff_norm[l])
        pre_a, post_a = row(attn_pre_norm[l]), row(attn_post_norm[l])
        pre_f, post_f = row(ffn_pre_norm[l]), row(ffn_post_norm[l])

        px = _in_proj(x, mx, pre_a, tabs, *proj_w, rope=True, with_q=True)
        pc = _in_proj(ctx, mc, pre_a, tabs, *proj_w, rope=False, with_q=not last)
        mkx, mvx, dkx, dvx, skx, svx, mqx, dqx, sqx, gx = px
        mkc, mvc, dkc, dvc, skc, svc = pc[:6]
        o_a = _mla(mqx, [mkc, mkx], [mvc, mvx])
        o_b = _diff(dqx, diff_lambda[l], dnorm, [dkc, dkx], [dvc, dvx], lam_init=lam_init)
        o_c = _swa(sqx, sink_cols, skc, svc, skx, svx)
        x = _merge(x, mx, post_a, o_a, o_b, o_c, gx, wa, wb, wc, wo)
        if not last:
            mqc, dqc, sqc, gc = pc[6:]
            p_a = _mla(mqc, [mkc], [mvc])
            p_b = _diff(dqc, diff_lambda[l], dnorm, [dkc], [dvc], lam_init=lam_init)
            p_c = _swa(sqc, sink_cols, skc, svc)
            ctx = _merge(ctx, mc, post_a, p_a, p_b, p_c, gc, wa, wb, wc, wo)

        x = _ffn(x, mx, pre_f, post_f, wgu, wd)
        if not last:
            ctx = _ffn(ctx, mc, pre_f, post_f, wgu, wd)
    return x
```

```python
import functools
import math

import jax
import jax.numpy as jnp
import numpy as np
from jax import lax
from jax.experimental import pallas as pl
from jax.experimental.pallas import tpu as pltpu

D_MODEL = 1024
GRID_W = 64
EPS = 1e-6
ROPE_BASE = 10000.0
LOG2E = math.log2(math.e)

MLA_HEADS = 8
MLA_RANK = 256
MLA_NOPE = 64
MLA_ROPE = 32
MLA_V = 64
MLA_SCALE = (MLA_NOPE + MLA_ROPE) ** -0.5

DIFF_HEADS = 4
DIFF_DH = 64
DIFF_SCALE = DIFF_DH ** -0.5

SWA_HEADS = 8
SWA_KV_HEADS = 2
SWA_DH = 64
SWA_GROUP = SWA_HEADS // SWA_KV_HEADS
WINDOW = 128
SWA_SCALE = SWA_DH ** -0.5

D_FF = 2816
N_MODS = 6

LANES = 128
BF16_ROWS = 16
VMEM_LIMIT = 56 * 1024 * 1024

ROW_TILE = 512
SUB_ROWS = 256
BOUND_ROWS = ('mla_q', 'diff_q', 'swa_q', 'mla_k', 'diff_k', 'swa_k', 'v_abs')
BF16_NORM2_SLACK = (1.0 + 2.0 ** -8) ** 4
Q_TILE = 512
Q_SUB = 256
KEY_CHUNK = 1024
SWA_BLOCKS = 4
LOGIT_BOUND = 64.0
VALUE_BOUND = 2.0 ** 40

_COLS = {}
_off = 0
for _name, _width in (('ckv', 256), ('cq', 256), ('kr', 128), ('dk', 512), ('sk', 128),
                      ('dq', 512), ('sq', 512), ('sv', 128), ('gates', 3072)):
    _COLS[_name] = (_off, _off + _width)
    _off += _width
IN_COLS = _off
SWA_HEAD_PERM = (0, 4, 1, 5, 2, 6, 3, 7)

BF16 = jnp.bfloat16
F32 = jnp.float32


def _dot(a, b):
    return jnp.dot(a, b, preferred_element_type=F32)


def _dot_nt(a, b):
    return lax.dot_general(a, b, (((1,), (1,)), ((), ())), preferred_element_type=F32)


def _rms(x, w):
    return x * lax.rsqrt(jnp.mean(x * x, axis=-1, keepdims=True) + EPS) * w


def _rope(x, cos, sin, first, shift):
    partner = jnp.where(first, pltpu.roll(x, LANES - shift, 1), pltpu.roll(x, shift, 1))
    return x * cos + partner * sin


def _const_spec(shape):
    nd = len(shape)
    return pl.BlockSpec(shape, lambda *_: (0,) * nd, pipeline_mode=pl.Buffered(1))


def _params(n_axes):
    return pltpu.CompilerParams(dimension_semantics=('arbitrary',) * n_axes,
                                vmem_limit_bytes=VMEM_LIMIT)


def _adaln_kernel(c_ref, w_ref, b_ref, o_ref):
    c = c_ref[...]
    s = c * jax.nn.sigmoid(c)
    o_ref[...] = _dot(s.astype(BF16), w_ref[...].astype(BF16)) + b_ref[...]


def _adaln(cond, w, b):
    rows, d = cond.shape
    n = w.shape[1]
    tn = n // 4
    return pl.pallas_call(
        _adaln_kernel,
        grid=(n // tn,),
        in_specs=[pl.BlockSpec((rows, d), lambda j: (0, 0)),
                  pl.BlockSpec((d, tn), lambda j: (0, j)),
                  pl.BlockSpec((1, tn), lambda j: (0, j))],
        out_specs=pl.BlockSpec((rows, tn), lambda j: (0, j)),
        out_shape=jax.ShapeDtypeStruct((rows, n), F32),
        compiler_params=_params(1),
        name='adaln',
    )(cond, w, b.reshape(1, n))


def _in_proj_kernel(x_ref, mod_ref, pre_ref, cos64_ref, sin64_ref, cosm_ref, sinm_ref,
                    w_ref, wdvt_ref, kvn_ref, wuk_ref, wuvt_ref, qn_ref, wuq_ref, *outs, rope, with_q):
    tm = x_ref.shape[1]
    sub = min(SUB_ROWS, tm)
    phases = [_in_proj_rows(slice(r0, r0 + sub), x_ref, mod_ref, pre_ref, cos64_ref, sin64_ref, cosm_ref,
                            sinm_ref, w_ref, wdvt_ref, kvn_ref, wuk_ref, wuvt_ref, qn_ref, wuq_ref, outs,
                            rope, with_q) for r0 in range(0, tm, sub)]
    next(phases[0])
    while phases:
        for rows in list(phases):
            try:
                next(rows)
            except StopIteration:
                phases.remove(rows)


def _chunk_norm2(z):
    return functools.reduce(jnp.maximum, [jnp.sum(z[:, i:i + LANES] * z[:, i:i + LANES], axis=1, keepdims=True)
                                          for i in range(0, z.shape[1], LANES)])


def _in_proj_rows(rs, x_ref, mod_ref, pre_ref, cos64_ref, sin64_ref, cosm_ref, sinm_ref,
                  w_ref, wdvt_ref, kvn_ref, wuk_ref, wuvt_ref, qn_ref, wuq_ref, outs, rope, with_q):
    x = x_ref[0, rs]
    mod = mod_ref[0]
    h = _rms(x, pre_ref[...]) * (1.0 + mod[1:2]) + mod[0:1]
    hb = h.astype(BF16)
    yield

    def proj(name):
        a, b = _COLS[name]
        return _dot(hb, w_ref[:, a:b])

    lane = lax.broadcasted_iota(jnp.int32, (x.shape[0], LANES), 1)
    first64 = (lane & 31) < 16
    firstm = (lane & 15) < 8
    if rope:
        cos64, sin64, cosm, sinm = cos64_ref[rs], sin64_ref[rs], cosm_ref[rs], sinm_ref[rs]

    def rope64(z, scale):
        if not rope:
            return z if scale == 1.0 else z * scale
        c, s = (cos64, sin64) if scale == 1.0 else (cos64 * scale, sin64 * scale)
        return jnp.concatenate(
            [_rope(z[:, i:i + LANES], c, s, first64, 16) for i in range(0, z.shape[1], LANES)], axis=1)

    (mk_ref, mvt_ref, dk_ref, dvt_ref, sk_ref, sv_ref), q_outs, bound_ref = outs[:6], outs[6:-1], outs[-1]
    norms = {}

    ckv_raw = proj('ckv')
    if with_q:
        cq_raw = proj('cq')
    kr = proj('kr')
    dk = rope64(proj('dk'), 1.0)
    dk_ref[0, rs] = dk.astype(BF16)
    norms['diff_k'] = _chunk_norm2(dk)
    yield
    ckv = _rms(ckv_raw, kvn_ref[...]).astype(BF16)
    dvt = _dot_nt(wdvt_ref[...], hb)
    dvt_ref[0, :, rs] = dvt.astype(BF16)
    yield
    k_nope = _dot(ckv, wuk_ref[...])
    if rope:
        kr = _rope(kr, cosm, sinm, firstm, 8)
    mk = jnp.concatenate([k_nope[:, i:i + LANES] + kr for i in range(0, MLA_HEADS * LANES, LANES)], axis=1)
    mk_ref[0, rs] = mk.astype(BF16)
    norms['mla_k'] = _chunk_norm2(mk)
    yield
    mvt = _dot_nt(wuvt_ref[...], ckv)
    mvt_ref[0, :, rs] = mvt.astype(BF16)
    sk = rope64(proj('sk'), 1.0)
    sk_ref[0, rs] = sk.astype(BF16)
    norms['swa_k'] = _chunk_norm2(sk)
    sv = proj('sv')
    sv_ref[0, rs] = sv.astype(BF16)
    yield

    if with_q:
        mq_ref, dq_ref, sq_ref, g_ref = q_outs
        cq = _rms(cq_raw, qn_ref[...]).astype(BF16)
        dq = rope64(proj('dq'), DIFF_SCALE * LOG2E)
        dq_ref[0, rs] = dq.astype(BF16)
        norms['diff_q'] = _chunk_norm2(dq)
        yield
        q = _dot(cq, wuq_ref[...])
        mla_scale = MLA_SCALE * LOG2E
        mq = []
        for i in range(0, MLA_HEADS * LANES, LANES):
            qi = q[:, i:i + LANES]
            mq.append(_rope(qi, cosm * mla_scale, sinm * mla_scale, firstm, 8) if rope else qi * mla_scale)
        mq = jnp.concatenate(mq, axis=1)
        mq_ref[0, rs] = mq.astype(BF16)
        norms['mla_q'] = _chunk_norm2(mq)
        yield
        sq = rope64(proj('sq'), SWA_SCALE * LOG2E)
        sq_ref[0, rs] = sq.astype(BF16)
        norms['swa_q'] = _chunk_norm2(sq)
        a, b = _COLS['gates']
        for i in range(a, b, 512):
            yield
            g_ref[0, rs, i - a:i - a + 512] = jax.nn.sigmoid(_dot(hb, w_ref[:, i:i + 512])).astype(BF16)

    vmax = functools.reduce(jnp.maximum, [jnp.max(jnp.max(jnp.abs(v), axis=1, keepdims=True), axis=0, keepdims=True)
                                          for v in (mvt, dvt, sv)])
    bounds = [jnp.max(norms[name], axis=0, keepdims=True) if name in norms else jnp.zeros((1, 1), F32)
              for name in BOUND_ROWS[:6]] + [vmax, jnp.zeros((1, 1), F32)]
    bound_ref[0, rs.start // (rs.stop - rs.start)] = jnp.broadcast_to(jnp.concatenate(bounds, axis=0), (8, LANES))


def _in_proj(x, mods, pre_w, tabs, wcat, wdvt, kvn, wuk, wuvt, qn, wuq, *, rope, with_q):
    bsz, n, d = x.shape
    tm = min(ROW_TILE, n)
    sub = min(SUB_ROWS, tm)
    per_batch_mod = mods.shape[0] > 1
    outs = [(MLA_HEADS * LANES, False), (MLA_HEADS * MLA_V, True), (512, False), (512, True),
            (128, False), (128, False)]
    if with_q:
        outs += [(MLA_HEADS * LANES, False), (512, False), (512, False), (3 * D_MODEL, False)]
    row_spec = lambda w: pl.BlockSpec((1, tm, w), lambda b, i: (b, i, 0))
    col_spec = lambda w: pl.BlockSpec((1, w, tm), lambda b, i: (b, 0, i))
    tab_spec = pl.BlockSpec((tm, LANES), lambda b, i: (i, 0))
    kern = functools.partial(_in_proj_kernel, rope=rope, with_q=with_q)
    return pl.pallas_call(
        kern,
        grid=(bsz, n // tm),
        in_specs=[row_spec(d),
                  pl.BlockSpec((1, N_MODS, d), (lambda b, i: (b, 0, 0)) if per_batch_mod else (lambda b, i: (0, 0, 0))),
                  _const_spec((1, d)),
                  tab_spec, tab_spec, tab_spec, tab_spec,
                  _const_spec(wcat.shape), _const_spec(wdvt.shape), _const_spec(kvn.shape),
                  _const_spec(wuk.shape), _const_spec(wuvt.shape), _const_spec(qn.shape), _const_spec(wuq.shape)],
        out_specs=[col_spec(w) if t else row_spec(w) for w, t in outs]
                  + [pl.BlockSpec((1, tm // sub, 8, LANES), lambda b, i: (b, i, 0, 0))],
        out_shape=[jax.ShapeDtypeStruct((bsz, w, n) if t else (bsz, n, w), BF16) for w, t in outs]
                  + [jax.ShapeDtypeStruct((bsz, n // sub, 8, LANES), F32)],
        compiler_params=_params(2),
        name='in_proj_rope' if rope else 'in_proj_ctx',
    )(x, mods, pre_w, *tabs, wcat, wdvt, kvn, wuk, wuvt, qn, wuq)


def _kv_chunks(k_refs, vt_refs, col, vrow, dv):
    chunks = []
    for k_ref, vt_ref in zip(k_refs, vt_refs):
        n = k_ref.shape[1]
        step = min(KEY_CHUNK, n)
        for c in range(0, n, step):
            chunks.append((functools.partial(lambda r, c, s: r[0, c:c + s, col:col + LANES], k_ref, c, step),
                           functools.partial(lambda r, c, s: r[0, vrow:vrow + dv, c:c + s], vt_ref, c, step)))
    return chunks


def _flash_chains(chains):
    n = len(chains[0][1])
    state = [(None, None)] * len(chains)
    s_next = [_dot_nt(chunks[0][0](), q) for q, chunks in chains]
    for i in range(n):
        for ci, (q, chunks) in enumerate(chains):
            m, acc = state[ci]
            s = s_next[ci]
            if i + 1 < n:
                s_next[ci] = _dot_nt(chunks[i + 1][0](), q)
            mc = jnp.max(s, axis=0, keepdims=True)
            m_new = mc if m is None else jnp.maximum(m, mc)
            pv = _dot(_with_ones_rows(chunks[i][1]()), jnp.exp2(s - m_new).astype(BF16))
            acc = pv if m is None else jnp.exp2(m - m_new) * acc + pv
            state[ci] = (m_new, acc)
    return [acc for _, acc in state]


def _unshifted_chains(chains):
    n = len(chains[0][1])
    acc = [None] * len(chains)
    s_next = [_dot_nt(chunks[0][0](), q) for q, chunks in chains]
    for i in range(n):
        for ci, (q, chunks) in enumerate(chains):
            s = s_next[ci]
            if i + 1 < n:
                s_next[ci] = _dot_nt(chunks[i + 1][0](), q)
            pv = _dot(_with_ones_rows(chunks[i][1]()), jnp.exp2(s).astype(BF16))
            acc[ci] = pv if acc[ci] is None else acc[ci] + pv
    return acc


def _with_ones_rows(vt):
    return jnp.concatenate([vt, jnp.ones((BF16_ROWS, vt.shape[1]), BF16)], axis=0)


def _run_chains(chains, bounded, finish):
    @pl.when(bounded)
    def _():
        finish(_unshifted_chains(chains))

    @pl.when(jnp.logical_not(bounded))
    def _():
        finish(_flash_chains(chains))


def _tile_bounded(bounded_ref, tile_axis):
    return bounded_ref[pl.program_id(0) * pl.num_programs(tile_axis) + pl.program_id(tile_axis)] != 0


def _prefetch_grid(grid, in_specs, out_specs):
    return pltpu.PrefetchScalarGridSpec(num_scalar_prefetch=1, grid=grid, in_specs=in_specs, out_specs=out_specs)


def _mla_kernel(bounded_ref, *refs, n_seg):
    q_ref = refs[0]
    k_refs = refs[1:1 + n_seg]
    vt_refs = refs[1 + n_seg:1 + 2 * n_seg]
    o_ref = refs[1 + 2 * n_seg]
    tq = q_ref.shape[1]
    subs = range(0, tq, Q_SUB)
    chains = []
    for j in range(2):
        chunks = _kv_chunks(k_refs, vt_refs, j * LANES, j * MLA_V, MLA_V)
        chains += [(q_ref[0, r:r + Q_SUB, j * LANES:(j + 1) * LANES], chunks) for r in subs]

    def finish(res):
        for ri, r in enumerate(subs):
            pair = [res[j * len(subs) + ri] for j in range(2)]
            o_t = jnp.concatenate([a[:MLA_V] / a[MLA_V:MLA_V + 1] for a in pair], axis=0)
            o_ref[0, r:r + Q_SUB, :] = o_t.T.astype(BF16)

    _run_chains(chains, _tile_bounded(bounded_ref, 2), finish)


def _mla(bounded, q, ks, vts):
    bsz, n, _ = q.shape
    tq = min(Q_TILE, n)
    n_seg = len(ks)
    return pl.pallas_call(
        functools.partial(_mla_kernel, n_seg=n_seg),
        grid_spec=_prefetch_grid(
            (bsz, MLA_HEADS // 2, n // tq),
            [pl.BlockSpec((1, tq, 2 * LANES), lambda b, h, i, _: (b, i, h))]
            + [pl.BlockSpec((1, k.shape[1], 2 * LANES), lambda b, h, i, _: (b, 0, h)) for k in ks]
            + [pl.BlockSpec((1, 2 * MLA_V, v.shape[2]), lambda b, h, i, _: (b, h, 0)) for v in vts],
            pl.BlockSpec((1, tq, LANES), lambda b, h, i, _: (b, i, h))),
        out_shape=jax.ShapeDtypeStruct((bsz, n, MLA_HEADS * MLA_V), BF16),
        compiler_params=_params(3),
        name=f'mla_attn_{n_seg}seg',
    )(bounded, q, *ks, *vts)


def _diff_kernel(bounded_ref, *refs, n_seg, lam_init):
    q_ref, lam_ref, nrm_ref = refs[:3]
    k_refs = refs[3:3 + n_seg]
    vt_refs = refs[3 + n_seg:3 + 2 * n_seg]
    o_ref = refs[3 + 2 * n_seg]
    lv = lam_ref[...]
    lam = (jnp.exp(jnp.sum(lv[0:1] * lv[1:2], axis=-1, keepdims=True))
           - jnp.exp(jnp.sum(lv[2:3] * lv[3:4], axis=-1, keepdims=True)) + lam_init)
    dv = 2 * DIFF_DH
    tq = q_ref.shape[1]
    subs = range(0, tq, Q_SUB)
    lane = lax.broadcasted_iota(jnp.int32, (Q_SUB, LANES), 1)
    zero = jnp.zeros((Q_SUB, LANES), BF16)
    chunks = _kv_chunks(k_refs, vt_refs, 0, 0, dv)
    chains = []
    for comp in range(2):
        keep = (lane < DIFF_DH) if comp == 0 else (lane >= DIFF_DH)
        chains += [(jnp.where(keep, q_ref[0, r:r + Q_SUB, :], zero), chunks) for r in subs]

    def finish(res):
        for ri, r in enumerate(subs):
            a0, a1 = res[ri], res[len(subs) + ri]
            o_t = a0[:dv] / a0[dv:dv + 1] - lam * (a1[:dv] / a1[dv:dv + 1])
            o_ref[0, r:r + Q_SUB, :] = (_rms(o_t.T, nrm_ref[...]) * (1.0 - lam_init)).astype(BF16)

    _run_chains(chains, _tile_bounded(bounded_ref, 2), finish)


def _diff(bounded, q, lam_w, nrm_w, ks, vts, *, lam_init):
    bsz, n, _ = q.shape
    tq = min(Q_TILE, n)
    n_seg = len(ks)
    const = lambda a: pl.BlockSpec(a.shape, lambda b, h, i, _: (0, 0), pipeline_mode=pl.Buffered(1))
    return pl.pallas_call(
        functools.partial(_diff_kernel, n_seg=n_seg, lam_init=lam_init),
        grid_spec=_prefetch_grid(
            (bsz, DIFF_HEADS, n // tq),
            [pl.BlockSpec((1, tq, LANES), lambda b, h, i, _: (b, i, h)), const(lam_w), const(nrm_w)]
            + [pl.BlockSpec((1, k.shape[1], LANES), lambda b, h, i, _: (b, 0, h)) for k in ks]
            + [pl.BlockSpec((1, 2 * DIFF_DH, v.shape[2]), lambda b, h, i, _: (b, h, 0)) for v in vts],
            pl.BlockSpec((1, tq, LANES), lambda b, h, i, _: (b, i, h))),
        out_shape=jax.ShapeDtypeStruct((bsz, n, DIFF_HEADS * 2 * DIFF_DH), BF16),
        compiler_params=_params(3),
        name=f'diff_attn_{n_seg}seg',
    )(bounded, q, lam_w, nrm_w, *ks, *vts)


def _swa_kernel(bounded_ref, *refs, local, seq):
    if local:
        q_ref, sink_ref, kc_ref, vc_ref, kx_ref, vx_ref, o_ref = refs
    else:
        q_ref, sink_ref, kc_ref, vc_ref, o_ref = refs
    blk = WINDOW
    n_blocks = q_ref.shape[1] // blk
    nb = seq // blk
    cols = SWA_GROUP * blk
    g = pl.program_id(1)
    sinks = [sink_ref[half] * LOG2E for half in range(SWA_KV_HEADS)]
    bounded = _tile_bounded(bounded_ref, 1)

    lane = lax.broadcasted_iota(jnp.int32, (blk, LANES), 1)
    kc = kc_ref[0]
    vct = vc_ref[0].T
    if local:
        key_r = lax.broadcasted_iota(jnp.int32, (blk, cols), 0)
        qry_r = lax.broadcasted_iota(jnp.int32, (blk, cols), 1) & (blk - 1)

        def block_rows(ref, idx):
            return ref[0, pl.ds(pl.multiple_of(idx * blk, blk), blk), :]

        first = g * n_blocks
        win = [jnp.clip(first - 1 + t, 0, nb - 1) for t in range(n_blocks + 2)]
        k_blocks = [block_rows(kx_ref, w) for w in win]
        vt_blocks = [block_rows(vx_ref, w).T for w in win]

    def run(shifted):
        for b in range(n_blocks):
            if local:
                i = first + b
                prev_vis = jnp.logical_and(qry_r <= key_r, i > 0)
                next_vis = jnp.logical_and(key_r <= qry_r, i < nb - 1)
                kl = jnp.concatenate(k_blocks[b:b + 3], axis=0)
                vlt = jnp.concatenate(vt_blocks[b:b + 3], axis=1)
            halves = []
            for half in range(SWA_KV_HEADS):
                keep = (lane < SWA_DH) if half == 0 else (lane >= SWA_DH)
                rows = slice(half * SWA_DH, (half + 1) * SWA_DH)
                q = jnp.concatenate(
                    [jnp.where(keep, q_ref[0, b * blk:(b + 1) * blk, c * LANES:(c + 1) * LANES],
                               jnp.zeros((blk, LANES), BF16)) for c in range(SWA_GROUP)], axis=0)
                sink = sinks[half]
                s_c = _dot_nt(kc, q)
                if local:
                    s_l = _dot_nt(kl, q)
                if shifted:
                    m = jnp.maximum(jnp.max(s_c, axis=0, keepdims=True), sink)
                    if local:
                        neg = -jnp.inf
                        m = jnp.maximum(m, jnp.max(jnp.where(prev_vis, s_l[:blk], neg), axis=0, keepdims=True))
                        m = jnp.maximum(m, jnp.max(s_l[blk:2 * blk], axis=0, keepdims=True))
                        m = jnp.maximum(m, jnp.max(jnp.where(next_vis, s_l[2 * blk:], neg), axis=0, keepdims=True))
                        s_l = s_l - m
                    s_c, sink = s_c - m, sink - m
                acc = _dot(_with_ones_rows(vct[rows]), jnp.exp2(s_c).astype(BF16))
                if local:
                    e_l = jnp.exp2(s_l)
                    p_l = jnp.concatenate([jnp.where(prev_vis, e_l[:blk], 0.0), e_l[blk:2 * blk],
                                           jnp.where(next_vis, e_l[2 * blk:], 0.0)], axis=0)
                    acc = acc + _dot(_with_ones_rows(vlt[rows]), p_l.astype(BF16))
                halves.append(acc[:SWA_DH] / (acc[SWA_DH:SWA_DH + 1] + jnp.exp2(sink)))
            o_t = jnp.concatenate(halves, axis=0)
            for c in range(SWA_GROUP):
                o_ref[0, b * blk:(b + 1) * blk, c * LANES:(c + 1) * LANES] = \
                    o_t[:, c * blk:(c + 1) * blk].T.astype(BF16)

    @pl.when(bounded)
    def _():
        run(False)

    @pl.when(jnp.logical_not(bounded))
    def _():
        run(True)


def _swa(bounded, q, sink_cols, kc, vc, kx=None, vx=None):
    bsz, n, _ = q.shape
    local = kx is not None
    rows = min(SWA_BLOCKS * WINDOW, n)
    full = lambda a: pl.BlockSpec((1,) + a.shape[1:], lambda b, i, _: (b, 0, 0))
    kv = [kc, vc] + ([kx, vx] if local else [])
    return pl.pallas_call(
        functools.partial(_swa_kernel, local=local, seq=n),
        grid_spec=_prefetch_grid(
            (bsz, n // rows),
            [pl.BlockSpec((1, rows, SWA_HEADS * SWA_DH), lambda b, i, _: (b, i, 0)),
             pl.BlockSpec(sink_cols.shape, lambda b, i, _: (0, 0, 0), pipeline_mode=pl.Buffered(1))]
            + [full(a) for a in kv],
            pl.BlockSpec((1, rows, SWA_HEADS * SWA_DH), lambda b, i, _: (b, i, 0))),
        out_shape=jax.ShapeDtypeStruct((bsz, n, SWA_HEADS * SWA_DH), BF16),
        compiler_params=_params(2),
        name='swa_attn_local' if local else 'swa_attn_ctx',
    )(bounded, q, sink_cols, *kv)


def _merge_kernel(x_ref, mod_ref, post_ref, oa_ref, ob_ref, oc_ref, g_ref,
                  wa_ref, wb_ref, wc_ref, wo_ref, o_ref):
    d = D_MODEL
    y = (g_ref[0, :, 0:d].astype(F32) * _dot(oa_ref[0], wa_ref[...])
         + g_ref[0, :, d:2 * d].astype(F32) * _dot(ob_ref[0], wb_ref[...])
         + g_ref[0, :, 2 * d:3 * d].astype(F32) * _dot(oc_ref[0], wc_ref[...]))
    y = _dot(y.astype(BF16), wo_ref[...])
    mod = mod_ref[0]
    o_ref[0] = x_ref[0] + mod[2:3] * _rms(y, post_ref[...])


def _merge(x, mods, post_w, oa, ob, oc, gates, wa, wb, wc, wo):
    bsz, n, d = x.shape
    tm = min(ROW_TILE, n)
    per_batch_mod = mods.shape[0] > 1
    row_spec = lambda w: pl.BlockSpec((1, tm, w), lambda b, i: (b, i, 0))
    return pl.pallas_call(
        _merge_kernel,
        grid=(bsz, n // tm),
        in_specs=[row_spec(d),
                  pl.BlockSpec((1, N_MODS, d), (lambda b, i: (b, 0, 0)) if per_batch_mod else (lambda b, i: (0, 0, 0))),
                  _const_spec((1, d)),
                  row_spec(512), row_spec(512), row_spec(512), row_spec(3 * d),
                  _const_spec(wa.shape), _const_spec(wb.shape), _const_spec(wc.shape), _const_spec(wo.shape)],
        out_specs=row_spec(d),
        out_shape=jax.ShapeDtypeStruct((bsz, n, d), F32),
        compiler_params=_params(2),
        name='merge_out_proj',
    )(x, mods, post_w, oa, ob, oc, gates, wa, wb, wc, wo)


def _ffn_kernel(x_ref, mod_ref, pre_ref, post_ref, wgu_ref, wd_ref, o_ref):
    tm = x_ref.shape[1]
    sub = min(SUB_ROWS, tm)
    mod = mod_ref[0]
    for r0 in range(0, tm, sub):
        rs = slice(r0, r0 + sub)
        x = x_ref[0, rs]
        h = (_rms(x, pre_ref[...]) * (1.0 + mod[4:5]) + mod[3:4]).astype(BF16)
        g = _dot(h, wgu_ref[:, 0:D_FF])
        u = _dot(h, wgu_ref[:, D_FF:2 * D_FF])
        act = (g * jax.nn.sigmoid(g) * u).astype(BF16)
        f = _dot(act, wd_ref[...])
        o_ref[0, rs] = x + mod[5:6] * _rms(f, post_ref[...])


def _ffn(x, mods, pre_w, post_w, wgu, wd):
    bsz, n, d = x.shape
    tm = min(ROW_TILE, n)
    per_batch_mod = mods.shape[0] > 1
    row_spec = pl.BlockSpec((1, tm, d), lambda b, i: (b, i, 0))
    return pl.pallas_call(
        _ffn_kernel,
        grid=(bsz, n // tm),
        in_specs=[row_spec,
                  pl.BlockSpec((1, N_MODS, d), (lambda b, i: (b, 0, 0)) if per_batch_mod else (lambda b, i: (0, 0, 0))),
                  _const_spec((1, d)), _const_spec((1, d)),
                  _const_spec(wgu.shape), _const_spec(wd.shape)],
        out_specs=row_spec,
        out_shape=jax.ShapeDtypeStruct((bsz, n, d), F32),
        compiler_params=_params(2),
        name='swiglu',
    )(x, mods, pre_w, post_w, wgu, wd)


def _rope_tables(seq):
    t = jnp.arange(seq)
    pos_row = (t // GRID_W).astype(F32)[:, None]
    pos_col = (t % GRID_W).astype(F32)[:, None]
    lanes = np.arange(LANES)

    def table(rot_dim, slot):
        axis_dim = rot_dim // 2
        half = axis_dim // 2
        inv = ROPE_BASE ** (-jnp.arange(0, axis_dim, 2, dtype=F32) / axis_dim)
        valid = slot >= 0
        axis, j = np.divmod(np.where(valid, slot, 0), axis_dim)
        ang = jnp.where(axis[None, :] == 0, pos_row, pos_col) * inv[j % half][None, :]
        sign = np.where(j < half, -1.0, 1.0).astype(np.float32)
        cos = jnp.where(valid[None, :], jnp.cos(ang), 1.0)
        sin = jnp.where(valid[None, :], jnp.sin(ang) * sign[None, :], 0.0)
        return cos, sin

    cos64, sin64 = table(SWA_DH, lanes % SWA_DH)
    in_rope = (lanes >= MLA_NOPE) & (lanes < MLA_NOPE + MLA_ROPE)
    cosm, sinm = table(MLA_ROPE, np.where(in_rope, lanes - MLA_NOPE, -1))
    return cos64, sin64, cosm, sinm


def _pack_in_proj(w):
    d = w.shape[0]
    ckv, kr, dk, dv, sk, sv, cq, dq, sq, gates = jnp.split(
        w, np.cumsum([256, 32, 512, 512, 128, 128, 256, 512, 512]).tolist(), axis=1)
    kr_wide = jnp.zeros((d, LANES), w.dtype).at[:, MLA_NOPE:MLA_NOPE + MLA_ROPE].set(kr)
    sq = sq.reshape(d, SWA_HEADS, SWA_DH)[:, np.array(SWA_HEAD_PERM)].reshape(d, SWA_HEADS * SWA_DH)
    parts = dict(ckv=ckv, cq=cq, kr=kr_wide, dk=dk, sk=sk, dq=dq, sq=sq, sv=sv, gates=gates)
    return jnp.concatenate([parts[name] for name in _COLS], axis=1).astype(BF16), dv.T.astype(BF16)


def _pad_heads(w, used):
    r = w.shape[0]
    w = w.reshape(r, MLA_HEADS, used)
    return jnp.pad(w, ((0, 0), (0, 0), (0, LANES - used))).reshape(r, MLA_HEADS * LANES)


def _bounded_tiles(bound_x, bound_c, sink):
    row = {name: i for i, name in enumerate(BOUND_ROWS)}
    bx, bc = bound_x[:, :, :, 0], bound_c[:, :, :, 0]
    keys = jnp.maximum(jnp.max(bx, axis=1), jnp.max(bc, axis=1))
    values_ok = keys[:, row['v_abs']] <= VALUE_BOUND
    sink_ok = jnp.max(jnp.abs(sink)) * LOG2E <= LOGIT_BOUND
    result = []
    for b, rows_per_sub in ((bx, SUB_ROWS), (bc, min(SUB_ROWS, bc.shape[1] * SUB_ROWS))):
        n = b.shape[1] * rows_per_sub
        per_mixer = {}
        for mixer, tile in (('mla', Q_TILE), ('diff', Q_TILE), ('swa', SWA_BLOCKS * WINDOW)):
            subs_per_tile = min(tile, n) // rows_per_sub
            q2 = jnp.max(b[:, :, row[mixer + '_q']].reshape(b.shape[0], -1, subs_per_tile), axis=2)
            ok = q2 * keys[:, None, row[mixer + '_k']] * BF16_NORM2_SLACK <= LOGIT_BOUND * LOGIT_BOUND
            ok = ok & values_ok[:, None] & (sink_ok if mixer == 'swa' else True)
            per_mixer[mixer] = ok.astype(jnp.int32).reshape(-1)
        result.append(per_mixer)
    return result


def kernel(x, c, ctx, c_ctx, w_ada, b_ada, attn_pre_norm, attn_post_norm, ffn_pre_norm, ffn_post_norm, w_in, mla_q_norm, w_uq, mla_kv_norm, w_ukv, diff_lambda, diff_norm, swa_sink, w_branch_a, w_branch_b, w_branch_c, w_o, w_gate_up, w_down):
    bsz, seq, d = x.shape
    depth = w_ada.shape[0]
    tabs = _rope_tables(seq)
    cond = jnp.concatenate([c, c_ctx[None, :], jnp.zeros((8 - bsz - 1, d), c.dtype)], axis=0)
    row = lambda v: v.reshape(1, -1)

    for l in range(depth):
        last = l == depth - 1
        lam_init = 0.8 - 0.6 * math.exp(-0.3 * l)
        mods = _adaln(cond, w_ada[l], b_ada[l])
        mx = mods[:bsz].reshape(bsz, N_MODS, d)
        mc = mods[bsz:bsz + 1].reshape(1, N_MODS, d)

        wcat, wdvt = _pack_in_proj(w_in[l])
        ukv = w_ukv[l].reshape(MLA_RANK, MLA_HEADS, MLA_NOPE + MLA_V)
        wuk = _pad_heads(ukv[:, :, :MLA_NOPE].reshape(MLA_RANK, -1), MLA_NOPE).astype(BF16)
        wuvt = ukv[:, :, MLA_NOPE:].reshape(MLA_RANK, MLA_HEADS * MLA_V).T.astype(BF16)
        wuq = _pad_heads(w_uq[l], MLA_NOPE + MLA_ROPE).astype(BF16)
        proj_w = (wcat, wdvt, row(mla_kv_norm[l]), wuk, wuvt, row(mla_q_norm[l]), wuq)
        sink_cols = jnp.repeat(swa_sink[l].reshape(SWA_KV_HEADS, 1, SWA_GROUP), WINDOW, axis=2)
        wa, wb = w_branch_a[l].astype(BF16), w_branch_b[l].astype(BF16)
        wc = w_branch_c[l].reshape(SWA_HEADS, SWA_DH, d)[np.array(SWA_HEAD_PERM)].reshape(-1, d).astype(BF16)
        wo, wgu, wd = w_o[l].astype(BF16), w_gate_up[l].astype(BF16), w_down[l].astype(BF16)
        dnorm = row(diff_norm[l])
        pre_a, post_a = row(attn_pre_norm[l]), row(attn_post_norm[l])
        pre_f, post_f = row(ffn_pre_norm[l]), row(ffn_post_norm[l])

        px = _in_proj(x, mx, pre_a, tabs, *proj_w, rope=True, with_q=True)
        pc = _in_proj(ctx, mc, pre_a, tabs, *proj_w, rope=False, with_q=not last)
        mkx, mvx, dkx, dvx, skx, svx, mqx, dqx, sqx, gx, bound_x = px
        mkc, mvc, dkc, dvc, skc, svc = pc[:6]
        bounded_x, bounded_c = _bounded_tiles(bound_x, pc[-1], swa_sink[l])
        o_a = _mla(bounded_x['mla'], mqx, [mkc, mkx], [mvc, mvx])
        o_b = _diff(bounded_x['diff'], dqx, diff_lambda[l], dnorm, [dkc, dkx], [dvc, dvx], lam_init=lam_init)
        o_c = _swa(bounded_x['swa'], sqx, sink_cols, skc, svc, skx, svx)
        x = _merge(x, mx, post_a, o_a, o_b, o_c, gx, wa, wb, wc, wo)
        if not last:
            mqc, dqc, sqc, gc = pc[6:-1]
            p_a = _mla(bounded_c['mla'], mqc, [mkc], [mvc])
            p_b = _diff(bounded_c['diff'], dqc, diff_lambda[l], dnorm, [dkc], [dvc], lam_init=lam_init)
            p_c = _swa(bounded_c['swa'], sqc, sink_cols, skc, svc)
            ctx = _merge(ctx, mc, post_a, p_a, p_b, p_c, gc, wa, wb, wc, wo)

        x = _ffn(x, mx, pre_f, post_f, wgu, wd)
        if not last:
            ctx = _ffn(ctx, mc, pre_f, post_f, wgu, wd)
    return x
```

```python
import functools
import math

import jax
import jax.numpy as jnp
import numpy as np
from jax import lax
from jax.experimental import pallas as pl
from jax.experimental.pallas import tpu as pltpu

D_MODEL = 1024
GRID_W = 64
EPS = 1e-6
ROPE_BASE = 10000.0
LOG2E = math.log2(math.e)

MLA_HEADS = 8
MLA_RANK = 256
MLA_NOPE = 64
MLA_ROPE = 32
MLA_V = 64
MLA_SCALE = (MLA_NOPE + MLA_ROPE) ** -0.5

DIFF_HEADS = 4
DIFF_DH = 64
DIFF_SCALE = DIFF_DH ** -0.5

SWA_HEADS = 8
SWA_KV_HEADS = 2
SWA_DH = 64
SWA_GROUP = SWA_HEADS // SWA_KV_HEADS
WINDOW = 128
SWA_SCALE = SWA_DH ** -0.5

D_FF = 2816
N_MODS = 6

LANES = 128
BF16_ROWS = 16
VMEM_LIMIT = 56 * 1024 * 1024

ROW_TILE = 512
SUB_ROWS = 256
BOUND_ROWS = ('mla_q', 'diff_q', 'swa_q', 'mla_k', 'diff_k', 'swa_k', 'v_abs')
BF16_NORM2_SLACK = (1.0 + 2.0 ** -8) ** 4
Q_TILE = 1024
Q_SUB = 256
KEY_CHUNK = 1024
SWA_BLOCKS = 4
LOGIT_BOUND = 64.0
VALUE_BOUND = 2.0 ** 40

_COLS = {}
_off = 0
for _name, _width in (('ckv', 256), ('cq', 256), ('kr', 128), ('dk', 512), ('sk', 128),
                      ('dq', 512), ('sq', 512), ('sv', 128), ('gates', 3072)):
    _COLS[_name] = (_off, _off + _width)
    _off += _width
IN_COLS = _off
SWA_HEAD_PERM = (0, 4, 1, 5, 2, 6, 3, 7)

BF16 = jnp.bfloat16
F32 = jnp.float32


def _dot(a, b):
    return jnp.dot(a, b, preferred_element_type=F32)


def _dot_nt(a, b):
    return lax.dot_general(a, b, (((1,), (1,)), ((), ())), preferred_element_type=F32)


def _rms(x, w):
    return x * lax.rsqrt(jnp.mean(x * x, axis=-1, keepdims=True) + EPS) * w


def _rope(x, cos, sin, first, shift):
    partner = jnp.where(first, pltpu.roll(x, LANES - shift, 1), pltpu.roll(x, shift, 1))
    return x * cos + partner * sin


def _const_spec(shape):
    nd = len(shape)
    return pl.BlockSpec(shape, lambda *_: (0,) * nd, pipeline_mode=pl.Buffered(1))


def _params(n_axes):
    return pltpu.CompilerParams(dimension_semantics=('arbitrary',) * n_axes,
                                vmem_limit_bytes=VMEM_LIMIT)


def _adaln_kernel(c_ref, w_ref, b_ref, o_ref):
    c = c_ref[...]
    s = c * jax.nn.sigmoid(c)
    o_ref[...] = _dot(s.astype(BF16), w_ref[...].astype(BF16)) + b_ref[...]


def _adaln(cond, w, b):
    rows, d = cond.shape
    n = w.shape[1]
    tn = n // 4
    return pl.pallas_call(
        _adaln_kernel,
        grid=(n // tn,),
        in_specs=[pl.BlockSpec((rows, d), lambda j: (0, 0)),
                  pl.BlockSpec((d, tn), lambda j: (0, j)),
                  pl.BlockSpec((1, tn), lambda j: (0, j))],
        out_specs=pl.BlockSpec((rows, tn), lambda j: (0, j)),
        out_shape=jax.ShapeDtypeStruct((rows, n), F32),
        compiler_params=_params(1),
        name='adaln',
    )(cond, w, b.reshape(1, n))


def _in_proj_kernel(x_ref, mod_ref, pre_ref, cos64_ref, sin64_ref, cosm_ref, sinm_ref,
                    w_ref, wdvt_ref, kvn_ref, wuk_ref, wuvt_ref, qn_ref, wuq_ref, *outs, rope, with_q):
    tm = x_ref.shape[1]
    sub = min(SUB_ROWS, tm)
    phases = [_in_proj_rows(slice(r0, r0 + sub), x_ref, mod_ref, pre_ref, cos64_ref, sin64_ref, cosm_ref,
                            sinm_ref, w_ref, wdvt_ref, kvn_ref, wuk_ref, wuvt_ref, qn_ref, wuq_ref, outs,
                            rope, with_q) for r0 in range(0, tm, sub)]
    next(phases[0])
    while phases:
        for rows in list(phases):
            try:
                next(rows)
            except StopIteration:
                phases.remove(rows)


def _chunk_norm2(z):
    return functools.reduce(jnp.maximum, [jnp.sum(z[:, i:i + LANES] * z[:, i:i + LANES], axis=1, keepdims=True)
                                          for i in range(0, z.shape[1], LANES)])


def _in_proj_rows(rs, x_ref, mod_ref, pre_ref, cos64_ref, sin64_ref, cosm_ref, sinm_ref,
                  w_ref, wdvt_ref, kvn_ref, wuk_ref, wuvt_ref, qn_ref, wuq_ref, outs, rope, with_q):
    x = x_ref[0, rs]
    mod = mod_ref[0]
    h = _rms(x, pre_ref[...]) * (1.0 + mod[1:2]) + mod[0:1]
    hb = h.astype(BF16)
    yield

    def proj(name):
        a, b = _COLS[name]
        return _dot(hb, w_ref[:, a:b])

    lane = lax.broadcasted_iota(jnp.int32, (x.shape[0], LANES), 1)
    first64 = (lane & 31) < 16
    firstm = (lane & 15) < 8
    if rope:
        cos64, sin64, cosm, sinm = cos64_ref[rs], sin64_ref[rs], cosm_ref[rs], sinm_ref[rs]

    def rope64(z, scale):
        if not rope:
            return z if scale == 1.0 else z * scale
        c, s = (cos64, sin64) if scale == 1.0 else (cos64 * scale, sin64 * scale)
        return jnp.concatenate(
            [_rope(z[:, i:i + LANES], c, s, first64, 16) for i in range(0, z.shape[1], LANES)], axis=1)

    (mk_ref, mvt_ref, dk_ref, dvt_ref, sk_ref, sv_ref), q_outs, bound_ref = outs[:6], outs[6:-1], outs[-1]
    norms = {}

    ckv_raw = proj('ckv')
    if with_q:
        cq_raw = proj('cq')
    kr = proj('kr')
    dk = rope64(proj('dk'), 1.0)
    dk_ref[0, rs] = dk.astype(BF16)
    norms['diff_k'] = _chunk_norm2(dk)
    yield
    ckv = _rms(ckv_raw, kvn_ref[...]).astype(BF16)
    dvt = _dot_nt(wdvt_ref[...], hb)
    dvt_ref[0, :, rs] = dvt.astype(BF16)
    yield
    k_nope = _dot(ckv, wuk_ref[...])
    if rope:
        kr = _rope(kr, cosm, sinm, firstm, 8)
    mk = jnp.concatenate([k_nope[:, i:i + LANES] + kr for i in range(0, MLA_HEADS * LANES, LANES)], axis=1)
    mk_ref[0, rs] = mk.astype(BF16)
    norms['mla_k'] = _chunk_norm2(mk)
    yield
    mvt = _dot_nt(wuvt_ref[...], ckv)
    mvt_ref[0, :, rs] = mvt.astype(BF16)
    sk = rope64(proj('sk'), 1.0)
    sk_ref[0, rs] = sk.astype(BF16)
    norms['swa_k'] = _chunk_norm2(sk)
    sv = proj('sv')
    sv_ref[0, rs] = sv.astype(BF16)
    yield

    if with_q:
        mq_ref, dq_ref, sq_ref, g_ref = q_outs
        cq = _rms(cq_raw, qn_ref[...]).astype(BF16)
        dq = rope64(proj('dq'), DIFF_SCALE * LOG2E)
        dq_ref[0, rs] = dq.astype(BF16)
        norms['diff_q'] = _chunk_norm2(dq)
        yield
        q = _dot(cq, wuq_ref[...])
        mla_scale = MLA_SCALE * LOG2E
        mq = []
        for i in range(0, MLA_HEADS * LANES, LANES):
            qi = q[:, i:i + LANES]
            mq.append(_rope(qi, cosm * mla_scale, sinm * mla_scale, firstm, 8) if rope else qi * mla_scale)
        mq = jnp.concatenate(mq, axis=1)
        mq_ref[0, rs] = mq.astype(BF16)
        norms['mla_q'] = _chunk_norm2(mq)
        yield
        sq = rope64(proj('sq'), SWA_SCALE * LOG2E)
        sq_ref[0, rs] = sq.astype(BF16)
        norms['swa_q'] = _chunk_norm2(sq)
        a, b = _COLS['gates']
        for i in range(a, b, 512):
            yield
            g_ref[0, rs, i - a:i - a + 512] = jax.nn.sigmoid(_dot(hb, w_ref[:, i:i + 512])).astype(BF16)

    vmax = functools.reduce(jnp.maximum, [jnp.max(jnp.max(jnp.abs(v), axis=1, keepdims=True), axis=0, keepdims=True)
                                          for v in (mvt, dvt, sv)])
    bounds = [jnp.max(norms[name], axis=0, keepdims=True) if name in norms else jnp.zeros((1, 1), F32)
              for name in BOUND_ROWS[:6]] + [vmax, jnp.zeros((1, 1), F32)]
    bound_ref[0, rs.start // (rs.stop - rs.start)] = jnp.broadcast_to(jnp.concatenate(bounds, axis=0), (8, LANES))


def _in_proj(x, mods, pre_w, tabs, wcat, wdvt, kvn, wuk, wuvt, qn, wuq, *, rope, with_q):
    bsz, n, d = x.shape
    tm = min(ROW_TILE, n)
    sub = min(SUB_ROWS, tm)
    per_batch_mod = mods.shape[0] > 1
    outs = [(MLA_HEADS * LANES, False), (MLA_HEADS * MLA_V, True), (512, False), (512, True),
            (128, False), (128, False)]
    if with_q:
        outs += [(MLA_HEADS * LANES, False), (512, False), (512, False), (3 * D_MODEL, False)]
    row_spec = lambda w: pl.BlockSpec((1, tm, w), lambda b, i: (b, i, 0))
    col_spec = lambda w: pl.BlockSpec((1, w, tm), lambda b, i: (b, 0, i))
    tab_spec = pl.BlockSpec((tm, LANES), lambda b, i: (i, 0))
    kern = functools.partial(_in_proj_kernel, rope=rope, with_q=with_q)
    return pl.pallas_call(
        kern,
        grid=(bsz, n // tm),
        in_specs=[row_spec(d),
                  pl.BlockSpec((1, N_MODS, d), (lambda b, i: (b, 0, 0)) if per_batch_mod else (lambda b, i: (0, 0, 0))),
                  _const_spec((1, d)),
                  tab_spec, tab_spec, tab_spec, tab_spec,
                  _const_spec(wcat.shape), _const_spec(wdvt.shape), _const_spec(kvn.shape),
                  _const_spec(wuk.shape), _const_spec(wuvt.shape), _const_spec(qn.shape), _const_spec(wuq.shape)],
        out_specs=[col_spec(w) if t else row_spec(w) for w, t in outs]
                  + [pl.BlockSpec((1, tm // sub, 8, LANES), lambda b, i: (b, i, 0, 0))],
        out_shape=[jax.ShapeDtypeStruct((bsz, w, n) if t else (bsz, n, w), BF16) for w, t in outs]
                  + [jax.ShapeDtypeStruct((bsz, n // sub, 8, LANES), F32)],
        compiler_params=_params(2),
        name='in_proj_rope' if rope else 'in_proj_ctx',
    )(x, mods, pre_w, *tabs, wcat, wdvt, kvn, wuk, wuvt, qn, wuq)


def _kv_chunks(k_refs, vt_refs, col, vrow, dv):
    chunks = []
    for k_ref, vt_ref in zip(k_refs, vt_refs):
        n = k_ref.shape[1]
        step = min(KEY_CHUNK, n)
        for c in range(0, n, step):
            chunks.append((functools.partial(lambda r, c, s: r[0, c:c + s, col:col + LANES], k_ref, c, step),
                           functools.partial(lambda r, c, s: r[0, vrow:vrow + dv, c:c + s], vt_ref, c, step)))
    return chunks


def _flash_chains(chains):
    n = len(chains[0][1])
    state = [(None, None)] * len(chains)
    s_next = [_dot_nt(chunks[0][0](), q) for q, chunks in chains]
    for i in range(n):
        for ci, (q, chunks) in enumerate(chains):
            m, acc = state[ci]
            s = s_next[ci]
            if i + 1 < n:
                s_next[ci] = _dot_nt(chunks[i + 1][0](), q)
            mc = jnp.max(s, axis=0, keepdims=True)
            m_new = mc if m is None else jnp.maximum(m, mc)
            pv = _dot(_with_ones_rows(chunks[i][1]()), jnp.exp2(s - m_new).astype(BF16))
            acc = pv if m is None else jnp.exp2(m - m_new) * acc + pv
            state[ci] = (m_new, acc)
    return [acc for _, acc in state]


def _unshifted_chains(chains):
    n = len(chains[0][1])
    acc = [None] * len(chains)
    s_next = [_dot_nt(chunks[0][0](), q) for q, chunks in chains]
    for i in range(n):
        for ci, (q, chunks) in enumerate(chains):
            s = s_next[ci]
            if i + 1 < n:
                s_next[ci] = _dot_nt(chunks[i + 1][0](), q)
            pv = _dot(_with_ones_rows(chunks[i][1]()), jnp.exp2(s).astype(BF16))
            acc[ci] = pv if acc[ci] is None else acc[ci] + pv
    return acc


def _with_ones_rows(vt):
    return jnp.concatenate([vt, jnp.ones((BF16_ROWS, vt.shape[1]), BF16)], axis=0)


def _run_chains(chains, bounded, finish):
    @pl.when(bounded)
    def _():
        finish(_unshifted_chains(chains))

    @pl.when(jnp.logical_not(bounded))
    def _():
        finish(_flash_chains(chains))


def _tile_bounded(bounded_ref, tile_axis):
    return bounded_ref[pl.program_id(0) * pl.num_programs(tile_axis) + pl.program_id(tile_axis)] != 0


def _prefetch_grid(grid, in_specs, out_specs):
    return pltpu.PrefetchScalarGridSpec(num_scalar_prefetch=1, grid=grid, in_specs=in_specs, out_specs=out_specs)


def _mla_kernel(bounded_ref, *refs, n_seg):
    q_ref = refs[0]
    k_refs = refs[1:1 + n_seg]
    vt_refs = refs[1 + n_seg:1 + 2 * n_seg]
    o_ref = refs[1 + 2 * n_seg]
    tq = q_ref.shape[1]
    subs = range(0, tq, Q_SUB)
    chains = []
    for j in range(2):
        chunks = _kv_chunks(k_refs, vt_refs, j * LANES, j * MLA_V, MLA_V)
        chains += [(q_ref[0, r:r + Q_SUB, j * LANES:(j + 1) * LANES], chunks) for r in subs]

    def finish(res):
        for ri, r in enumerate(subs):
            pair = [res[j * len(subs) + ri] for j in range(2)]
            o_t = jnp.concatenate([a[:MLA_V] / a[MLA_V:MLA_V + 1] for a in pair], axis=0)
            o_ref[0, r:r + Q_SUB, :] = o_t.T.astype(BF16)

    _run_chains(chains, _tile_bounded(bounded_ref, 2), finish)


def _mla(bounded, q, ks, vts):
    bsz, n, _ = q.shape
    tq = min(Q_TILE, n)
    n_seg = len(ks)
    return pl.pallas_call(
        functools.partial(_mla_kernel, n_seg=n_seg),
        grid_spec=_prefetch_grid(
            (bsz, MLA_HEADS // 2, n // tq),
            [pl.BlockSpec((1, tq, 2 * LANES), lambda b, h, i, _: (b, i, h))]
            + [pl.BlockSpec((1, k.shape[1], 2 * LANES), lambda b, h, i, _: (b, 0, h)) for k in ks]
            + [pl.BlockSpec((1, 2 * MLA_V, v.shape[2]), lambda b, h, i, _: (b, h, 0)) for v in vts],
            pl.BlockSpec((1, tq, LANES), lambda b, h, i, _: (b, i, h))),
        out_shape=jax.ShapeDtypeStruct((bsz, n, MLA_HEADS * MLA_V), BF16),
        compiler_params=_params(3),
        name=f'mla_attn_{n_seg}seg',
    )(bounded, q, *ks, *vts)


def _diff_kernel(bounded_ref, *refs, n_seg, lam_init):
    q_ref, lam_ref, nrm_ref = refs[:3]
    k_refs = refs[3:3 + n_seg]
    vt_refs = refs[3 + n_seg:3 + 2 * n_seg]
    o_ref = refs[3 + 2 * n_seg]
    lv = lam_ref[...]
    lam = (jnp.exp(jnp.sum(lv[0:1] * lv[1:2], axis=-1, keepdims=True))
           - jnp.exp(jnp.sum(lv[2:3] * lv[3:4], axis=-1, keepdims=True)) + lam_init)
    dv = 2 * DIFF_DH
    tq = q_ref.shape[1]
    subs = range(0, tq, Q_SUB)
    lane = lax.broadcasted_iota(jnp.int32, (Q_SUB, LANES), 1)
    zero = jnp.zeros((Q_SUB, LANES), BF16)
    chunks = _kv_chunks(k_refs, vt_refs, 0, 0, dv)
    chains = []
    for comp in range(2):
        keep = (lane < DIFF_DH) if comp == 0 else (lane >= DIFF_DH)
        chains += [(jnp.where(keep, q_ref[0, r:r + Q_SUB, :], zero), chunks) for r in subs]

    def finish(res):
        for ri, r in enumerate(subs):
            a0, a1 = res[ri], res[len(subs) + ri]
            o_t = a0[:dv] / a0[dv:dv + 1] - lam * (a1[:dv] / a1[dv:dv + 1])
            o_ref[0, r:r + Q_SUB, :] = (_rms(o_t.T, nrm_ref[...]) * (1.0 - lam_init)).astype(BF16)

    _run_chains(chains, _tile_bounded(bounded_ref, 2), finish)


def _diff(bounded, q, lam_w, nrm_w, ks, vts, *, lam_init):
    bsz, n, _ = q.shape
    tq = min(Q_TILE, n)
    n_seg = len(ks)
    const = lambda a: pl.BlockSpec(a.shape, lambda b, h, i, _: (0, 0), pipeline_mode=pl.Buffered(1))
    return pl.pallas_call(
        functools.partial(_diff_kernel, n_seg=n_seg, lam_init=lam_init),
        grid_spec=_prefetch_grid(
            (bsz, DIFF_HEADS, n // tq),
            [pl.BlockSpec((1, tq, LANES), lambda b, h, i, _: (b, i, h)), const(lam_w), const(nrm_w)]
            + [pl.BlockSpec((1, k.shape[1], LANES), lambda b, h, i, _: (b, 0, h)) for k in ks]
            + [pl.BlockSpec((1, 2 * DIFF_DH, v.shape[2]), lambda b, h, i, _: (b, h, 0)) for v in vts],
            pl.BlockSpec((1, tq, LANES), lambda b, h, i, _: (b, i, h))),
        out_shape=jax.ShapeDtypeStruct((bsz, n, DIFF_HEADS * 2 * DIFF_DH), BF16),
        compiler_params=_params(3),
        name=f'diff_attn_{n_seg}seg',
    )(bounded, q, lam_w, nrm_w, *ks, *vts)


def _swa_kernel(bounded_ref, *refs, local, seq):
    if local:
        q_ref, sink_ref, kc_ref, vc_ref, kx_ref, vx_ref, o_ref = refs
    else:
        q_ref, sink_ref, kc_ref, vc_ref, o_ref = refs
    blk = WINDOW
    n_blocks = q_ref.shape[1] // blk
    nb = seq // blk
    cols = SWA_GROUP * blk
    g = pl.program_id(1)
    sinks = [sink_ref[half] * LOG2E for half in range(SWA_KV_HEADS)]
    bounded = _tile_bounded(bounded_ref, 1)

    lane = lax.broadcasted_iota(jnp.int32, (blk, LANES), 1)
    kc = kc_ref[0]
    vct = vc_ref[0].T
    if local:
        key_r = lax.broadcasted_iota(jnp.int32, (blk, cols), 0)
        qry_r = lax.broadcasted_iota(jnp.int32, (blk, cols), 1) & (blk - 1)

        def block_rows(ref, idx):
            return ref[0, pl.ds(pl.multiple_of(idx * blk, blk), blk), :]

        first = g * n_blocks
        win = [jnp.clip(first - 1 + t, 0, nb - 1) for t in range(n_blocks + 2)]
        k_blocks = [block_rows(kx_ref, w) for w in win]
        vt_blocks = [block_rows(vx_ref, w).T for w in win]

    def run(shifted):
        for b in range(n_blocks):
            if local:
                i = first + b
                prev_vis = jnp.logical_and(qry_r <= key_r, i > 0)
                next_vis = jnp.logical_and(key_r <= qry_r, i < nb - 1)
                kl = jnp.concatenate(k_blocks[b:b + 3], axis=0)
                vlt = jnp.concatenate(vt_blocks[b:b + 3], axis=1)
            halves = []
            for half in range(SWA_KV_HEADS):
                keep = (lane < SWA_DH) if half == 0 else (lane >= SWA_DH)
                rows = slice(half * SWA_DH, (half + 1) * SWA_DH)
                q = jnp.concatenate(
                    [jnp.where(keep, q_ref[0, b * blk:(b + 1) * blk, c * LANES:(c + 1) * LANES],
                               jnp.zeros((blk, LANES), BF16)) for c in range(SWA_GROUP)], axis=0)
                sink = sinks[half]
                s_c = _dot_nt(kc, q)
                if local:
                    s_l = _dot_nt(kl, q)
                if shifted:
                    m = jnp.maximum(jnp.max(s_c, axis=0, keepdims=True), sink)
                    if local:
                        neg = -jnp.inf
                        m = jnp.maximum(m, jnp.max(jnp.where(prev_vis, s_l[:blk], neg), axis=0, keepdims=True))
                        m = jnp.maximum(m, jnp.max(s_l[blk:2 * blk], axis=0, keepdims=True))
                        m = jnp.maximum(m, jnp.max(jnp.where(next_vis, s_l[2 * blk:], neg), axis=0, keepdims=True))
                        s_l = s_l - m
                    s_c, sink = s_c - m, sink - m
                acc = _dot(_with_ones_rows(vct[rows]), jnp.exp2(s_c).astype(BF16))
                if local:
                    e_l = jnp.exp2(s_l)
                    p_l = jnp.concatenate([jnp.where(prev_vis, e_l[:blk], 0.0), e_l[blk:2 * blk],
                                           jnp.where(next_vis, e_l[2 * blk:], 0.0)], axis=0)
                    acc = acc + _dot(_with_ones_rows(vlt[rows]), p_l.astype(BF16))
                halves.append(acc[:SWA_DH] / (acc[SWA_DH:SWA_DH + 1] + jnp.exp2(sink)))
            o_t = jnp.concatenate(halves, axis=0)
            for c in range(SWA_GROUP):
                o_ref[0, b * blk:(b + 1) * blk, c * LANES:(c + 1) * LANES] = \
                    o_t[:, c * blk:(c + 1) * blk].T.astype(BF16)

    @pl.when(bounded)
    def _():
        run(False)

    @pl.when(jnp.logical_not(bounded))
    def _():
        run(True)


def _swa(bounded, q, sink_cols, kc, vc, kx=None, vx=None):
    bsz, n, _ = q.shape
    local = kx is not None
    rows = min(SWA_BLOCKS * WINDOW, n)
    full = lambda a: pl.BlockSpec((1,) + a.shape[1:], lambda b, i, _: (b, 0, 0))
    kv = [kc, vc] + ([kx, vx] if local else [])
    return pl.pallas_call(
        functools.partial(_swa_kernel, local=local, seq=n),
        grid_spec=_prefetch_grid(
            (bsz, n // rows),
            [pl.BlockSpec((1, rows, SWA_HEADS * SWA_DH), lambda b, i, _: (b, i, 0)),
             pl.BlockSpec(sink_cols.shape, lambda b, i, _: (0, 0, 0), pipeline_mode=pl.Buffered(1))]
            + [full(a) for a in kv],
            pl.BlockSpec((1, rows, SWA_HEADS * SWA_DH), lambda b, i, _: (b, i, 0))),
        out_shape=jax.ShapeDtypeStruct((bsz, n, SWA_HEADS * SWA_DH), BF16),
        compiler_params=_params(2),
        name='swa_attn_local' if local else 'swa_attn_ctx',
    )(bounded, q, sink_cols, *kv)


def _merge_kernel(x_ref, mod_ref, post_ref, oa_ref, ob_ref, oc_ref, g_ref,
                  wa_ref, wb_ref, wc_ref, wo_ref, o_ref):
    d = D_MODEL
    y = (g_ref[0, :, 0:d].astype(F32) * _dot(oa_ref[0], wa_ref[...])
         + g_ref[0, :, d:2 * d].astype(F32) * _dot(ob_ref[0], wb_ref[...])
         + g_ref[0, :, 2 * d:3 * d].astype(F32) * _dot(oc_ref[0], wc_ref[...]))
    y = _dot(y.astype(BF16), wo_ref[...])
    mod = mod_ref[0]
    o_ref[0] = x_ref[0] + mod[2:3] * _rms(y, post_ref[...])


def _merge(x, mods, post_w, oa, ob, oc, gates, wa, wb, wc, wo):
    bsz, n, d = x.shape
    tm = min(ROW_TILE, n)
    per_batch_mod = mods.shape[0] > 1
    row_spec = lambda w: pl.BlockSpec((1, tm, w), lambda b, i: (b, i, 0))
    return pl.pallas_call(
        _merge_kernel,
        grid=(bsz, n // tm),
        in_specs=[row_spec(d),
                  pl.BlockSpec((1, N_MODS, d), (lambda b, i: (b, 0, 0)) if per_batch_mod else (lambda b, i: (0, 0, 0))),
                  _const_spec((1, d)),
                  row_spec(512), row_spec(512), row_spec(512), row_spec(3 * d),
                  _const_spec(wa.shape), _const_spec(wb.shape), _const_spec(wc.shape), _const_spec(wo.shape)],
        out_specs=row_spec(d),
        out_shape=jax.ShapeDtypeStruct((bsz, n, d), F32),
        compiler_params=_params(2),
        name='merge_out_proj',
    )(x, mods, post_w, oa, ob, oc, gates, wa, wb, wc, wo)


def _ffn_kernel(x_ref, mod_ref, pre_ref, post_ref, wgu_ref, wd_ref, o_ref):
    tm = x_ref.shape[1]
    sub = min(SUB_ROWS, tm)
    mod = mod_ref[0]
    for r0 in range(0, tm, sub):
        rs = slice(r0, r0 + sub)
        x = x_ref[0, rs]
        h = (_rms(x, pre_ref[...]) * (1.0 + mod[4:5]) + mod[3:4]).astype(BF16)
        g = _dot(h, wgu_ref[:, 0:D_FF])
        u = _dot(h, wgu_ref[:, D_FF:2 * D_FF])
        act = (g * jax.nn.sigmoid(g) * u).astype(BF16)
        f = _dot(act, wd_ref[...])
        o_ref[0, rs] = x + mod[5:6] * _rms(f, post_ref[...])


def _ffn(x, mods, pre_w, post_w, wgu, wd):
    bsz, n, d = x.shape
    tm = min(ROW_TILE, n)
    per_batch_mod = mods.shape[0] > 1
    row_spec = pl.BlockSpec((1, tm, d), lambda b, i: (b, i, 0))
    return pl.pallas_call(
        _ffn_kernel,
        grid=(bsz, n // tm),
        in_specs=[row_spec,
                  pl.BlockSpec((1, N_MODS, d), (lambda b, i: (b, 0, 0)) if per_batch_mod else (lambda b, i: (0, 0, 0))),
                  _const_spec((1, d)), _const_spec((1, d)),
                  _const_spec(wgu.shape), _const_spec(wd.shape)],
        out_specs=row_spec,
        out_shape=jax.ShapeDtypeStruct((bsz, n, d), F32),
        compiler_params=_params(2),
        name='swiglu',
    )(x, mods, pre_w, post_w, wgu, wd)


def _rope_tables(seq):
    grid_rows = seq // GRID_W
    lanes = np.arange(LANES)

    def table(rot_dim, slot):
        axis_dim = rot_dim // 2
        half = axis_dim // 2
        inv = (ROPE_BASE ** (-jnp.arange(0, axis_dim, 2, dtype=F32) / axis_dim))
        valid = slot >= 0
        axis, j = np.divmod(np.where(valid, slot, 0), axis_dim)
        freq = inv[j % half][None, :]
        ang_row = jnp.arange(grid_rows, dtype=F32)[:, None] * freq
        ang_col = jnp.arange(GRID_W, dtype=F32)[:, None] * freq
        sign = np.where(j < half, -1.0, 1.0).astype(np.float32)
        on_row = (axis == 0)[None, None, :]
        keep = valid[None, None, :]

        def spread(fn, fill, scale):
            full = jnp.where(on_row, (fn(ang_row) * scale)[:, None, :], (fn(ang_col) * scale)[None, :, :])
            return jnp.where(keep, full, fill).reshape(seq, LANES)

        return spread(jnp.cos, 1.0, 1.0), spread(jnp.sin, 0.0, sign[None, :])

    cos64, sin64 = table(SWA_DH, lanes % SWA_DH)
    in_rope = (lanes >= MLA_NOPE) & (lanes < MLA_NOPE + MLA_ROPE)
    cosm, sinm = table(MLA_ROPE, np.where(in_rope, lanes - MLA_NOPE, -1))
    return cos64, sin64, cosm, sinm


def _pack_in_proj(w):
    d = w.shape[0]
    ckv, kr, dk, dv, sk, sv, cq, dq, sq, gates = jnp.split(
        w, np.cumsum([256, 32, 512, 512, 128, 128, 256, 512, 512]).tolist(), axis=1)
    kr_wide = jnp.zeros((d, LANES), w.dtype).at[:, MLA_NOPE:MLA_NOPE + MLA_ROPE].set(kr)
    sq = sq.reshape(d, SWA_HEADS, SWA_DH)[:, np.array(SWA_HEAD_PERM)].reshape(d, SWA_HEADS * SWA_DH)
    parts = dict(ckv=ckv, cq=cq, kr=kr_wide, dk=dk, sk=sk, dq=dq, sq=sq, sv=sv, gates=gates)
    return jnp.concatenate([parts[name] for name in _COLS], axis=1).astype(BF16), dv.T.astype(BF16)


def _pad_heads(w, used):
    r = w.shape[0]
    w = w.reshape(r, MLA_HEADS, used)
    return jnp.pad(w, ((0, 0), (0, 0), (0, LANES - used))).reshape(r, MLA_HEADS * LANES)


def _bounded_tiles(bound_x, bound_c, sink):
    row = {name: i for i, name in enumerate(BOUND_ROWS)}
    bx, bc = bound_x[:, :, :, 0], bound_c[:, :, :, 0]
    keys = jnp.maximum(jnp.max(bx, axis=1), jnp.max(bc, axis=1))
    values_ok = keys[:, row['v_abs']] <= VALUE_BOUND
    sink_ok = jnp.max(jnp.abs(sink)) * LOG2E <= LOGIT_BOUND
    result = []
    for b, rows_per_sub in ((bx, SUB_ROWS), (bc, min(SUB_ROWS, bc.shape[1] * SUB_ROWS))):
        n = b.shape[1] * rows_per_sub
        per_mixer = {}
        for mixer, tile in (('mla', Q_TILE), ('diff', Q_TILE), ('swa', SWA_BLOCKS * WINDOW)):
            subs_per_tile = min(tile, n) // rows_per_sub
            q2 = jnp.max(b[:, :, row[mixer + '_q']].reshape(b.shape[0], -1, subs_per_tile), axis=2)
            ok = q2 * keys[:, None, row[mixer + '_k']] * BF16_NORM2_SLACK <= LOGIT_BOUND * LOGIT_BOUND
            ok = ok & values_ok[:, None] & (sink_ok if mixer == 'swa' else True)
            per_mixer[mixer] = ok.astype(jnp.int32).reshape(-1)
        result.append(per_mixer)
    return result


def kernel(x, c, ctx, c_ctx, w_ada, b_ada, attn_pre_norm, attn_post_norm, ffn_pre_norm, ffn_post_norm, w_in, mla_q_norm, w_uq, mla_kv_norm, w_ukv, diff_lambda, diff_norm, swa_sink, w_branch_a, w_branch_b, w_branch_c, w_o, w_gate_up, w_down):
    bsz, seq, d = x.shape
    depth = w_ada.shape[0]
    tabs = _rope_tables(seq)
    cond = jnp.concatenate([c, c_ctx[None, :], jnp.zeros((8 - bsz - 1, d), c.dtype)], axis=0)
    row = lambda v: v.reshape(1, -1)

    for l in range(depth):
        last = l == depth - 1
        lam_init = 0.8 - 0.6 * math.exp(-0.3 * l)
        mods = _adaln(cond, w_ada[l], b_ada[l])
        mx = mods[:bsz].reshape(bsz, N_MODS, d)
        mc = mods[bsz:bsz + 1].reshape(1, N_MODS, d)

        wcat, wdvt = _pack_in_proj(w_in[l])
        ukv = w_ukv[l].reshape(MLA_RANK, MLA_HEADS, MLA_NOPE + MLA_V)
        wuk = _pad_heads(ukv[:, :, :MLA_NOPE].reshape(MLA_RANK, -1), MLA_NOPE).astype(BF16)
        wuvt = ukv[:, :, MLA_NOPE:].reshape(MLA_RANK, MLA_HEADS * MLA_V).T.astype(BF16)
        wuq = _pad_heads(w_uq[l], MLA_NOPE + MLA_ROPE).astype(BF16)
        proj_w = (wcat, wdvt, row(mla_kv_norm[l]), wuk, wuvt, row(mla_q_norm[l]), wuq)
        sink_cols = jnp.repeat(swa_sink[l].reshape(SWA_KV_HEADS, 1, SWA_GROUP), WINDOW, axis=2)
        wa, wb = w_branch_a[l].astype(BF16), w_branch_b[l].astype(BF16)
        wc = w_branch_c[l].reshape(SWA_HEADS, SWA_DH, d)[np.array(SWA_HEAD_PERM)].reshape(-1, d).astype(BF16)
        wo, wgu, wd = w_o[l].astype(BF16), w_gate_up[l].astype(BF16), w_down[l].astype(BF16)
        dnorm = row(diff_norm[l])
        pre_a, post_a = row(attn_pre_norm[l]), row(attn_post_norm[l])
        pre_f, post_f = row(ffn_pre_norm[l]), row(ffn_post_norm[l])

        px = _in_proj(x, mx, pre_a, tabs, *proj_w, rope=True, with_q=True)
        pc = _in_proj(ctx, mc, pre_a, tabs, *proj_w, rope=False, with_q=not last)
        mkx, mvx, dkx, dvx, skx, svx, mqx, dqx, sqx, gx, bound_x = px
        mkc, mvc, dkc, dvc, skc, svc = pc[:6]
        bounded_x, bounded_c = _bounded_tiles(bound_x, pc[-1], swa_sink[l])
        o_a = _mla(bounded_x['mla'], mqx, [mkc, mkx], [mvc, mvx])
        o_b = _diff(bounded_x['diff'], dqx, diff_lambda[l], dnorm, [dkc, dkx], [dvc, dvx], lam_init=lam_init)
        o_c = _swa(bounded_x['swa'], sqx, sink_cols, skc, svc, skx, svx)
        x = _merge(x, mx, post_a, o_a, o_b, o_c, gx, wa, wb, wc, wo)
        if not last:
            mqc, dqc, sqc, gc = pc[6:-1]
            p_a = _mla(bounded_c['mla'], mqc, [mkc], [mvc])
            p_b = _diff(bounded_c['diff'], dqc, diff_lambda[l], dnorm, [dkc], [dvc], lam_init=lam_init)
            p_c = _swa(bounded_c['swa'], sqc, sink_cols, skc, svc)
            ctx = _merge(ctx, mc, post_a, p_a, p_b, p_c, gc, wa, wb, wc, wo)

        x = _ffn(x, mx, pre_f, post_f, wgu, wd)
        if not last:
            ctx = _ffn(ctx, mc, pre_f, post_f, wgu, wd)
    return x
```

```python
import functools
import math

import jax
import jax.numpy as jnp
import numpy as np
from jax import lax
from jax.experimental import pallas as pl
from jax.experimental.pallas import tpu as pltpu

D_MODEL = 1024
GRID_W = 64
EPS = 1e-6
ROPE_BASE = 10000.0
LOG2E = math.log2(math.e)

MLA_HEADS = 8
MLA_RANK = 256
MLA_NOPE = 64
MLA_ROPE = 32
MLA_V = 64
MLA_SCALE = (MLA_NOPE + MLA_ROPE) ** -0.5

DIFF_HEADS = 4
DIFF_DH = 64
DIFF_SCALE = DIFF_DH ** -0.5

SWA_HEADS = 8
SWA_KV_HEADS = 2
SWA_DH = 64
SWA_GROUP = SWA_HEADS // SWA_KV_HEADS
WINDOW = 128
SWA_SCALE = SWA_DH ** -0.5

D_FF = 2816
N_MODS = 6

LANES = 128
BF16_ROWS = 16
VMEM_LIMIT = 56 * 1024 * 1024

ROW_TILE = 512
FF_TILE = 1024
SUB_ROWS = 256
BOUND_ROWS = ('mla_q', 'diff_q', 'swa_q', 'mla_k', 'diff_k', 'swa_k', 'v_abs')
BF16_NORM2_SLACK = (1.0 + 2.0 ** -8) ** 4
Q_TILE = 1024
Q_SUB = 256
KEY_CHUNK = 1024
SWA_BLOCKS = 8
LOGIT_BOUND = 64.0
VALUE_BOUND = 2.0 ** 40

_COLS = {}
_off = 0
for _name, _width in (('ckv', 256), ('cq', 256), ('kr', 128), ('dk', 512), ('sk', 128),
                      ('dq', 512), ('sq', 512), ('sv', 128), ('gates', 3072)):
    _COLS[_name] = (_off, _off + _width)
    _off += _width
IN_COLS = _off
SWA_HEAD_PERM = (0, 4, 1, 5, 2, 6, 3, 7)

BF16 = jnp.bfloat16
F32 = jnp.float32


def _dot(a, b):
    return jnp.dot(a, b, preferred_element_type=F32)


def _dot_nt(a, b):
    return lax.dot_general(a, b, (((1,), (1,)), ((), ())), preferred_element_type=F32)


def _rms(x, w):
    return x * lax.rsqrt(jnp.mean(x * x, axis=-1, keepdims=True) + EPS) * w


def _rope(x, cos, sin, first, shift):
    partner = jnp.where(first, pltpu.roll(x, LANES - shift, 1), pltpu.roll(x, shift, 1))
    return x * cos + partner * sin


def _const_spec(shape):
    nd = len(shape)
    return pl.BlockSpec(shape, lambda *_: (0,) * nd, pipeline_mode=pl.Buffered(1))


def _params(n_axes):
    return pltpu.CompilerParams(dimension_semantics=('arbitrary',) * n_axes,
                                vmem_limit_bytes=VMEM_LIMIT)


def _adaln_kernel(c_ref, w_ref, b_ref, o_ref):
    c = c_ref[...]
    s = c * jax.nn.sigmoid(c)
    o_ref[...] = _dot(s.astype(BF16), w_ref[...].astype(BF16)) + b_ref[...]


def _adaln(cond, w, b):
    rows, d = cond.shape
    n = w.shape[1]
    tn = n // 4
    return pl.pallas_call(
        _adaln_kernel,
        grid=(n // tn,),
        in_specs=[pl.BlockSpec((rows, d), lambda j: (0, 0)),
                  pl.BlockSpec((d, tn), lambda j: (0, j)),
                  pl.BlockSpec((1, tn), lambda j: (0, j))],
        out_specs=pl.BlockSpec((rows, tn), lambda j: (0, j)),
        out_shape=jax.ShapeDtypeStruct((rows, n), F32),
        compiler_params=_params(1),
        name='adaln',
    )(cond, w, b.reshape(1, n))


def _in_proj_kernel(x_ref, mod_ref, pre_ref, cos64_ref, sin64_ref, cosm_ref, sinm_ref,
                    w_ref, wdvt_ref, kvn_ref, wuk_ref, wuvt_ref, qn_ref, wuq_ref, *outs, rope, with_q):
    tm = x_ref.shape[1]
    sub = min(SUB_ROWS, tm)
    phases = [_in_proj_rows(slice(r0, r0 + sub), x_ref, mod_ref, pre_ref, cos64_ref, sin64_ref, cosm_ref,
                            sinm_ref, w_ref, wdvt_ref, kvn_ref, wuk_ref, wuvt_ref, qn_ref, wuq_ref, outs,
                            rope, with_q) for r0 in range(0, tm, sub)]
    next(phases[0])
    while phases:
        for rows in list(phases):
            try:
                next(rows)
            except StopIteration:
                phases.remove(rows)


def _chunk_norm2(z):
    return functools.reduce(jnp.maximum, [jnp.sum(z[:, i:i + LANES] * z[:, i:i + LANES], axis=1, keepdims=True)
                                          for i in range(0, z.shape[1], LANES)])


def _in_proj_rows(rs, x_ref, mod_ref, pre_ref, cos64_ref, sin64_ref, cosm_ref, sinm_ref,
                  w_ref, wdvt_ref, kvn_ref, wuk_ref, wuvt_ref, qn_ref, wuq_ref, outs, rope, with_q):
    x = x_ref[0, rs]
    mod = mod_ref[0]
    h = _rms(x, pre_ref[...]) * (1.0 + mod[1:2]) + mod[0:1]
    hb = h.astype(BF16)
    yield

    def proj(name):
        a, b = _COLS[name]
        return _dot(hb, w_ref[:, a:b])

    lane = lax.broadcasted_iota(jnp.int32, (x.shape[0], LANES), 1)
    first64 = (lane & 31) < 16
    firstm = (lane & 15) < 8
    if rope:
        cos64, sin64, cosm, sinm = cos64_ref[rs], sin64_ref[rs], cosm_ref[rs], sinm_ref[rs]

    def rope64(z, scale):
        if not rope:
            return z if scale == 1.0 else z * scale
        c, s = (cos64, sin64) if scale == 1.0 else (cos64 * scale, sin64 * scale)
        return jnp.concatenate(
            [_rope(z[:, i:i + LANES], c, s, first64, 16) for i in range(0, z.shape[1], LANES)], axis=1)

    (mk_ref, mvt_ref, dk_ref, dvt_ref, sk_ref, sv_ref), q_outs, bound_ref = outs[:6], outs[6:-1], outs[-1]
    norms = {}

    ckv_raw = proj('ckv')
    if with_q:
        cq_raw = proj('cq')
    kr = proj('kr')
    dk = rope64(proj('dk'), 1.0)
    dk_ref[0, rs] = dk.astype(BF16)
    norms['diff_k'] = _chunk_norm2(dk)
    yield
    ckv = _rms(ckv_raw, kvn_ref[...]).astype(BF16)
    dvt = _dot_nt(wdvt_ref[...], hb)
    dvt_ref[0, :, rs] = dvt.astype(BF16)
    yield
    k_nope = _dot(ckv, wuk_ref[...])
    if rope:
        kr = _rope(kr, cosm, sinm, firstm, 8)
    mk = jnp.concatenate([k_nope[:, i:i + LANES] + kr for i in range(0, MLA_HEADS * LANES, LANES)], axis=1)
    mk_ref[0, rs] = mk.astype(BF16)
    norms['mla_k'] = _chunk_norm2(mk)
    yield
    mvt = _dot_nt(wuvt_ref[...], ckv)
    mvt_ref[0, :, rs] = mvt.astype(BF16)
    sk = rope64(proj('sk'), 1.0)
    sk_ref[0, rs] = sk.astype(BF16)
    norms['swa_k'] = _chunk_norm2(sk)
    sv = proj('sv')
    sv_ref[0, rs] = sv.astype(BF16)
    yield

    if with_q:
        mq_ref, dq_ref, sq_ref, g_ref = q_outs
        cq = _rms(cq_raw, qn_ref[...]).astype(BF16)
        dq = rope64(proj('dq'), DIFF_SCALE * LOG2E)
        dq_ref[0, rs] = dq.astype(BF16)
        norms['diff_q'] = _chunk_norm2(dq)
        yield
        q = _dot(cq, wuq_ref[...])
        mla_scale = MLA_SCALE * LOG2E
        mq = []
        for i in range(0, MLA_HEADS * LANES, LANES):
            qi = q[:, i:i + LANES]
            mq.append(_rope(qi, cosm * mla_scale, sinm * mla_scale, firstm, 8) if rope else qi * mla_scale)
        mq = jnp.concatenate(mq, axis=1)
        mq_ref[0, rs] = mq.astype(BF16)
        norms['mla_q'] = _chunk_norm2(mq)
        yield
        sq = rope64(proj('sq'), SWA_SCALE * LOG2E)
        sq_ref[0, rs] = sq.astype(BF16)
        norms['swa_q'] = _chunk_norm2(sq)
        a, b = _COLS['gates']
        for i in range(a, b, 512):
            yield
            g_ref[0, rs, i - a:i - a + 512] = jax.nn.sigmoid(_dot(hb, w_ref[:, i:i + 512])).astype(BF16)

    vmax = functools.reduce(jnp.maximum, [jnp.max(jnp.max(jnp.abs(v), axis=1, keepdims=True), axis=0, keepdims=True)
                                          for v in (mvt, dvt, sv)])
    bounds = [jnp.max(norms[name], axis=0, keepdims=True) if name in norms else jnp.zeros((1, 1), F32)
              for name in BOUND_ROWS[:6]] + [vmax, jnp.zeros((1, 1), F32)]
    bound_ref[0, rs.start // (rs.stop - rs.start)] = jnp.broadcast_to(jnp.concatenate(bounds, axis=0), (8, LANES))


def _in_proj(x, mods, pre_w, tabs, wcat, wdvt, kvn, wuk, wuvt, qn, wuq, *, rope, with_q):
    bsz, n, d = x.shape
    tm = min(ROW_TILE, n)
    sub = min(SUB_ROWS, tm)
    per_batch_mod = mods.shape[0] > 1
    outs = [(MLA_HEADS * LANES, False), (MLA_HEADS * MLA_V, True), (512, False), (512, True),
            (128, False), (128, False)]
    if with_q:
        outs += [(MLA_HEADS * LANES, False), (512, False), (512, False), (3 * D_MODEL, False)]
    row_spec = lambda w: pl.BlockSpec((1, tm, w), lambda b, i: (b, i, 0))
    col_spec = lambda w: pl.BlockSpec((1, w, tm), lambda b, i: (b, 0, i))
    tab_spec = pl.BlockSpec((tm, LANES), lambda b, i: (i, 0))
    kern = functools.partial(_in_proj_kernel, rope=rope, with_q=with_q)
    return pl.pallas_call(
        kern,
        grid=(bsz, n // tm),
        in_specs=[row_spec(d),
                  pl.BlockSpec((1, N_MODS, d), (lambda b, i: (b, 0, 0)) if per_batch_mod else (lambda b, i: (0, 0, 0))),
                  _const_spec((1, d)),
                  tab_spec, tab_spec, tab_spec, tab_spec,
                  _const_spec(wcat.shape), _const_spec(wdvt.shape), _const_spec(kvn.shape),
                  _const_spec(wuk.shape), _const_spec(wuvt.shape), _const_spec(qn.shape), _const_spec(wuq.shape)],
        out_specs=[col_spec(w) if t else row_spec(w) for w, t in outs]
                  + [pl.BlockSpec((1, tm // sub, 8, LANES), lambda b, i: (b, i, 0, 0))],
        out_shape=[jax.ShapeDtypeStruct((bsz, w, n) if t else (bsz, n, w), BF16) for w, t in outs]
                  + [jax.ShapeDtypeStruct((bsz, n // sub, 8, LANES), F32)],
        compiler_params=_params(2),
        name='in_proj_rope' if rope else 'in_proj_ctx',
    )(x, mods, pre_w, *tabs, wcat, wdvt, kvn, wuk, wuvt, qn, wuq)


def _kv_chunks(k_refs, vt_refs, col, vrow, dv):
    chunks = []
    for k_ref, vt_ref in zip(k_refs, vt_refs):
        n = k_ref.shape[1]
        step = min(KEY_CHUNK, n)
        for c in range(0, n, step):
            chunks.append((functools.partial(lambda r, c, s: r[0, c:c + s, col:col + LANES], k_ref, c, step),
                           functools.partial(lambda r, c, s: r[0, vrow:vrow + dv, c:c + s], vt_ref, c, step)))
    return chunks


def _flash_chains(chains):
    n = len(chains[0][1])
    state = [(None, None)] * len(chains)
    s_next = [_dot_nt(chunks[0][0](), q) for q, chunks in chains]
    for i in range(n):
        for ci, (q, chunks) in enumerate(chains):
            m, acc = state[ci]
            s = s_next[ci]
            if i + 1 < n:
                s_next[ci] = _dot_nt(chunks[i + 1][0](), q)
            mc = jnp.max(s, axis=0, keepdims=True)
            m_new = mc if m is None else jnp.maximum(m, mc)
            pv = _dot(_with_ones_rows(chunks[i][1]()), jnp.exp2(s - m_new).astype(BF16))
            acc = pv if m is None else jnp.exp2(m - m_new) * acc + pv
            state[ci] = (m_new, acc)
    return [acc for _, acc in state]


def _unshifted_chains(chains):
    n = len(chains[0][1])
    acc = [None] * len(chains)
    s_next = [_dot_nt(chunks[0][0](), q) for q, chunks in chains]
    for i in range(n):
        for ci, (q, chunks) in enumerate(chains):
            s = s_next[ci]
            if i + 1 < n:
                s_next[ci] = _dot_nt(chunks[i + 1][0](), q)
            pv = _dot(_with_ones_rows(chunks[i][1]()), jnp.exp2(s).astype(BF16))
            acc[ci] = pv if acc[ci] is None else acc[ci] + pv
    return acc


def _with_ones_rows(vt):
    return jnp.concatenate([vt, jnp.ones((BF16_ROWS, vt.shape[1]), BF16)], axis=0)


def _run_chains(chains, bounded, finish):
    @pl.when(bounded)
    def _():
        finish(_unshifted_chains(chains))

    @pl.when(jnp.logical_not(bounded))
    def _():
        finish(_flash_chains(chains))


def _tile_bounded(bounded_ref, tile_axis):
    return bounded_ref[pl.program_id(0) * pl.num_programs(tile_axis) + pl.program_id(tile_axis)] != 0


def _prefetch_grid(grid, in_specs, out_specs):
    return pltpu.PrefetchScalarGridSpec(num_scalar_prefetch=1, grid=grid, in_specs=in_specs, out_specs=out_specs)


def _mla_kernel(bounded_ref, *refs, n_seg):
    q_ref = refs[0]
    k_refs = refs[1:1 + n_seg]
    vt_refs = refs[1 + n_seg:1 + 2 * n_seg]
    o_ref = refs[1 + 2 * n_seg]
    tq = q_ref.shape[1]
    subs = range(0, tq, Q_SUB)
    chains = []
    for j in range(2):
        chunks = _kv_chunks(k_refs, vt_refs, j * LANES, j * MLA_V, MLA_V)
        chains += [(q_ref[0, r:r + Q_SUB, j * LANES:(j + 1) * LANES], chunks) for r in subs]

    def finish(res):
        for ri, r in enumerate(subs):
            pair = [res[j * len(subs) + ri] for j in range(2)]
            o_t = jnp.concatenate([a[:MLA_V] / a[MLA_V:MLA_V + 1] for a in pair], axis=0)
            o_ref[0, r:r + Q_SUB, :] = o_t.T.astype(BF16)

    _run_chains(chains, _tile_bounded(bounded_ref, 2), finish)


def _mla(bounded, q, ks, vts):
    bsz, n, _ = q.shape
    tq = min(Q_TILE, n)
    n_seg = len(ks)
    return pl.pallas_call(
        functools.partial(_mla_kernel, n_seg=n_seg),
        grid_spec=_prefetch_grid(
            (bsz, MLA_HEADS // 2, n // tq),
            [pl.BlockSpec((1, tq, 2 * LANES), lambda b, h, i, _: (b, i, h))]
            + [pl.BlockSpec((1, k.shape[1], 2 * LANES), lambda b, h, i, _: (b, 0, h)) for k in ks]
            + [pl.BlockSpec((1, 2 * MLA_V, v.shape[2]), lambda b, h, i, _: (b, h, 0)) for v in vts],
            pl.BlockSpec((1, tq, LANES), lambda b, h, i, _: (b, i, h))),
        out_shape=jax.ShapeDtypeStruct((bsz, n, MLA_HEADS * MLA_V), BF16),
        compiler_params=_params(3),
        name=f'mla_attn_{n_seg}seg',
    )(bounded, q, *ks, *vts)


def _diff_kernel(bounded_ref, *refs, n_seg, lam_init):
    q_ref, lam_ref, nrm_ref = refs[:3]
    k_refs = refs[3:3 + n_seg]
    vt_refs = refs[3 + n_seg:3 + 2 * n_seg]
    o_ref = refs[3 + 2 * n_seg]
    lv = lam_ref[...]
    lam = (jnp.exp(jnp.sum(lv[0:1] * lv[1:2], axis=-1, keepdims=True))
           - jnp.exp(jnp.sum(lv[2:3] * lv[3:4], axis=-1, keepdims=True)) + lam_init)
    dv = 2 * DIFF_DH
    tq = q_ref.shape[1]
    subs = range(0, tq, Q_SUB)
    lane = lax.broadcasted_iota(jnp.int32, (Q_SUB, LANES), 1)
    zero = jnp.zeros((Q_SUB, LANES), BF16)
    chunks = _kv_chunks(k_refs, vt_refs, 0, 0, dv)
    chains = []
    for comp in range(2):
        keep = (lane < DIFF_DH) if comp == 0 else (lane >= DIFF_DH)
        chains += [(jnp.where(keep, q_ref[0, r:r + Q_SUB, :], zero), chunks) for r in subs]

    def finish(res):
        for ri, r in enumerate(subs):
            a0, a1 = res[ri], res[len(subs) + ri]
            o_t = a0[:dv] / a0[dv:dv + 1] - lam * (a1[:dv] / a1[dv:dv + 1])
            o_ref[0, r:r + Q_SUB, :] = (_rms(o_t.T, nrm_ref[...]) * (1.0 - lam_init)).astype(BF16)

    _run_chains(chains, _tile_bounded(bounded_ref, 2), finish)


def _diff(bounded, q, lam_w, nrm_w, ks, vts, *, lam_init):
    bsz, n, _ = q.shape
    tq = min(Q_TILE, n)
    n_seg = len(ks)
    const = lambda a: pl.BlockSpec(a.shape, lambda b, h, i, _: (0, 0), pipeline_mode=pl.Buffered(1))
    return pl.pallas_call(
        functools.partial(_diff_kernel, n_seg=n_seg, lam_init=lam_init),
        grid_spec=_prefetch_grid(
            (bsz, DIFF_HEADS, n // tq),
            [pl.BlockSpec((1, tq, LANES), lambda b, h, i, _: (b, i, h)), const(lam_w), const(nrm_w)]
            + [pl.BlockSpec((1, k.shape[1], LANES), lambda b, h, i, _: (b, 0, h)) for k in ks]
            + [pl.BlockSpec((1, 2 * DIFF_DH, v.shape[2]), lambda b, h, i, _: (b, h, 0)) for v in vts],
            pl.BlockSpec((1, tq, LANES), lambda b, h, i, _: (b, i, h))),
        out_shape=jax.ShapeDtypeStruct((bsz, n, DIFF_HEADS * 2 * DIFF_DH), BF16),
        compiler_params=_params(3),
        name=f'diff_attn_{n_seg}seg',
    )(bounded, q, lam_w, nrm_w, *ks, *vts)


def _swa_kernel(bounded_ref, *refs, local, seq):
    if local:
        q_ref, sink_ref, kc_ref, vc_ref, kx_ref, vx_ref, o_ref = refs
    else:
        q_ref, sink_ref, kc_ref, vc_ref, o_ref = refs
    blk = WINDOW
    n_blocks = q_ref.shape[1] // blk
    nb = seq // blk
    cols = SWA_GROUP * blk
    g = pl.program_id(1)
    sinks = [sink_ref[half] * LOG2E for half in range(SWA_KV_HEADS)]
    bounded = _tile_bounded(bounded_ref, 1)

    lane = lax.broadcasted_iota(jnp.int32, (blk, LANES), 1)
    kc = kc_ref[0]
    vct = vc_ref[0].T
    if local:
        key_r = lax.broadcasted_iota(jnp.int32, (blk, cols), 0)
        qry_r = lax.broadcasted_iota(jnp.int32, (blk, cols), 1) & (blk - 1)

        def block_rows(ref, idx):
            return ref[0, pl.ds(pl.multiple_of(idx * blk, blk), blk), :]

        first = g * n_blocks
        win = [jnp.clip(first - 1 + t, 0, nb - 1) for t in range(n_blocks + 2)]
        k_blocks = [block_rows(kx_ref, w) for w in win]
        vt_blocks = [block_rows(vx_ref, w).T for w in win]

    def run(shifted):
        pairs = [(b, half) for b in range(n_blocks) for half in range(SWA_KV_HEADS)]
        window = {}
        if local:
            for b in range(n_blocks):
                i = first + b
                window[b] = (jnp.logical_and(qry_r <= key_r, i > 0), jnp.logical_and(key_r <= qry_r, i < nb - 1),
                             jnp.concatenate(k_blocks[b:b + 3], axis=0),
                             jnp.concatenate(vt_blocks[b:b + 3], axis=1))
        scores = {}
        for b, half in pairs:
            keep = (lane < SWA_DH) if half == 0 else (lane >= SWA_DH)
            q = jnp.concatenate(
                [jnp.where(keep, q_ref[0, b * blk:(b + 1) * blk, c * LANES:(c + 1) * LANES],
                           jnp.zeros((blk, LANES), BF16)) for c in range(SWA_GROUP)], axis=0)
            scores[b, half] = (_dot_nt(kc, q), _dot_nt(window[b][2], q) if local else None)
        probs = {}
        for b, half in pairs:
            s_c, s_l = scores[b, half]
            sink = sinks[half]
            if local:
                prev_vis, next_vis = window[b][:2]
            if shifted:
                m = jnp.maximum(jnp.max(s_c, axis=0, keepdims=True), sink)
                if local:
                    neg = -jnp.inf
                    m = jnp.maximum(m, jnp.max(jnp.where(prev_vis, s_l[:blk], neg), axis=0, keepdims=True))
                    m = jnp.maximum(m, jnp.max(s_l[blk:2 * blk], axis=0, keepdims=True))
                    m = jnp.maximum(m, jnp.max(jnp.where(next_vis, s_l[2 * blk:], neg), axis=0, keepdims=True))
                    s_l = s_l - m
                s_c, sink = s_c - m, sink - m
            p_l = None
            if local:
                e_l = jnp.exp2(s_l)
                p_l = jnp.concatenate([jnp.where(prev_vis, e_l[:blk], 0.0), e_l[blk:2 * blk],
                                       jnp.where(next_vis, e_l[2 * blk:], 0.0)], axis=0).astype(BF16)
            probs[b, half] = (jnp.exp2(s_c).astype(BF16), p_l, jnp.exp2(sink))
        outs = {}
        for b, half in pairs:
            p_c, p_l, e_sink = probs[b, half]
            rows = slice(half * SWA_DH, (half + 1) * SWA_DH)
            acc = _dot(_with_ones_rows(vct[rows]), p_c)
            if local:
                acc = acc + _dot(_with_ones_rows(window[b][3][rows]), p_l)
            outs[b, half] = acc[:SWA_DH] / (acc[SWA_DH:SWA_DH + 1] + e_sink)
        for b in range(n_blocks):
            o_t = jnp.concatenate([outs[b, half] for half in range(SWA_KV_HEADS)], axis=0)
            for c in range(SWA_GROUP):
                o_ref[0, b * blk:(b + 1) * blk, c * LANES:(c + 1) * LANES] = \
                    o_t[:, c * blk:(c + 1) * blk].T.astype(BF16)

    @pl.when(bounded)
    def _():
        run(False)

    @pl.when(jnp.logical_not(bounded))
    def _():
        run(True)


def _swa(bounded, q, sink_cols, kc, vc, kx=None, vx=None):
    bsz, n, _ = q.shape
    local = kx is not None
    rows = min(SWA_BLOCKS * WINDOW, n)
    full = lambda a: pl.BlockSpec((1,) + a.shape[1:], lambda b, i, _: (b, 0, 0))
    kv = [kc, vc] + ([kx, vx] if local else [])
    return pl.pallas_call(
        functools.partial(_swa_kernel, local=local, seq=n),
        grid_spec=_prefetch_grid(
            (bsz, n // rows),
            [pl.BlockSpec((1, rows, SWA_HEADS * SWA_DH), lambda b, i, _: (b, i, 0)),
             pl.BlockSpec(sink_cols.shape, lambda b, i, _: (0, 0, 0), pipeline_mode=pl.Buffered(1))]
            + [full(a) for a in kv],
            pl.BlockSpec((1, rows, SWA_HEADS * SWA_DH), lambda b, i, _: (b, i, 0))),
        out_shape=jax.ShapeDtypeStruct((bsz, n, SWA_HEADS * SWA_DH), BF16),
        compiler_params=_params(2),
        name='swa_attn_local' if local else 'swa_attn_ctx',
    )(bounded, q, sink_cols, *kv)


def _merge_kernel(x_ref, mod_ref, post_ref, oa_ref, ob_ref, oc_ref, g_ref,
                  wa_ref, wb_ref, wc_ref, wo_ref, o_ref):
    d = D_MODEL
    y = (g_ref[0, :, 0:d].astype(F32) * _dot(oa_ref[0], wa_ref[...])
         + g_ref[0, :, d:2 * d].astype(F32) * _dot(ob_ref[0], wb_ref[...])
         + g_ref[0, :, 2 * d:3 * d].astype(F32) * _dot(oc_ref[0], wc_ref[...]))
    y = _dot(y.astype(BF16), wo_ref[...])
    mod = mod_ref[0]
    o_ref[0] = x_ref[0] + mod[2:3] * _rms(y, post_ref[...])


def _merge(x, mods, post_w, oa, ob, oc, gates, wa, wb, wc, wo):
    bsz, n, d = x.shape
    tm = min(FF_TILE, n)
    per_batch_mod = mods.shape[0] > 1
    row_spec = lambda w: pl.BlockSpec((1, tm, w), lambda b, i: (b, i, 0))
    return pl.pallas_call(
        _merge_kernel,
        grid=(bsz, n // tm),
        in_specs=[row_spec(d),
                  pl.BlockSpec((1, N_MODS, d), (lambda b, i: (b, 0, 0)) if per_batch_mod else (lambda b, i: (0, 0, 0))),
                  _const_spec((1, d)),
                  row_spec(512), row_spec(512), row_spec(512), row_spec(3 * d),
                  _const_spec(wa.shape), _const_spec(wb.shape), _const_spec(wc.shape), _const_spec(wo.shape)],
        out_specs=row_spec(d),
        out_shape=jax.ShapeDtypeStruct((bsz, n, d), F32),
        compiler_params=_params(2),
        name='merge_out_proj',
    )(x, mods, post_w, oa, ob, oc, gates, wa, wb, wc, wo)


def _ffn_kernel(x_ref, mod_ref, pre_ref, post_ref, wgu_ref, wd_ref, o_ref):
    tm = x_ref.shape[1]
    sub = min(SUB_ROWS, tm)
    mod = mod_ref[0]
    for r0 in range(0, tm, sub):
        rs = slice(r0, r0 + sub)
        x = x_ref[0, rs]
        h = (_rms(x, pre_ref[...]) * (1.0 + mod[4:5]) + mod[3:4]).astype(BF16)
        g = _dot(h, wgu_ref[:, 0:D_FF])
        u = _dot(h, wgu_ref[:, D_FF:2 * D_FF])
        act = (g * jax.nn.sigmoid(g) * u).astype(BF16)
        f = _dot(act, wd_ref[...])
        o_ref[0, rs] = x + mod[5:6] * _rms(f, post_ref[...])


def _ffn(x, mods, pre_w, post_w, wgu, wd):
    bsz, n, d = x.shape
    tm = min(FF_TILE, n)
    per_batch_mod = mods.shape[0] > 1
    row_spec = pl.BlockSpec((1, tm, d), lambda b, i: (b, i, 0))
    return pl.pallas_call(
        _ffn_kernel,
        grid=(bsz, n // tm),
        in_specs=[row_spec,
                  pl.BlockSpec((1, N_MODS, d), (lambda b, i: (b, 0, 0)) if per_batch_mod else (lambda b, i: (0, 0, 0))),
                  _const_spec((1, d)), _const_spec((1, d)),
                  _const_spec(wgu.shape), _const_spec(wd.shape)],
        out_specs=row_spec,
        out_shape=jax.ShapeDtypeStruct((bsz, n, d), F32),
        compiler_params=_params(2),
        name='swiglu',
    )(x, mods, pre_w, post_w, wgu, wd)


def _rope_tables(seq):
    grid_rows = seq // GRID_W
    lanes = np.arange(LANES)

    def table(rot_dim, slot):
        axis_dim = rot_dim // 2
        half = axis_dim // 2
        inv = (ROPE_BASE ** (-jnp.arange(0, axis_dim, 2, dtype=F32) / axis_dim))
        valid = slot >= 0
        axis, j = np.divmod(np.where(valid, slot, 0), axis_dim)
        freq = inv[j % half][None, :]
        ang_row = jnp.arange(grid_rows, dtype=F32)[:, None] * freq
        ang_col = jnp.arange(GRID_W, dtype=F32)[:, None] * freq
        sign = np.where(j < half, -1.0, 1.0).astype(np.float32)
        on_row = (axis == 0)[None, None, :]
        keep = valid[None, None, :]

        def spread(fn, fill, scale):
            full = jnp.where(on_row, (fn(ang_row) * scale)[:, None, :], (fn(ang_col) * scale)[None, :, :])
            return jnp.where(keep, full, fill).reshape(seq, LANES)

        return spread(jnp.cos, 1.0, 1.0), spread(jnp.sin, 0.0, sign[None, :])

    cos64, sin64 = table(SWA_DH, lanes % SWA_DH)
    in_rope = (lanes >= MLA_NOPE) & (lanes < MLA_NOPE + MLA_ROPE)
    cosm, sinm = table(MLA_ROPE, np.where(in_rope, lanes - MLA_NOPE, -1))
    return cos64, sin64, cosm, sinm


def _pack_in_proj(w):
    d = w.shape[0]
    ckv, kr, dk, dv, sk, sv, cq, dq, sq, gates = jnp.split(
        w, np.cumsum([256, 32, 512, 512, 128, 128, 256, 512, 512]).tolist(), axis=1)
    kr_wide = jnp.zeros((d, LANES), w.dtype).at[:, MLA_NOPE:MLA_NOPE + MLA_ROPE].set(kr)
    sq = sq.reshape(d, SWA_HEADS, SWA_DH)[:, np.array(SWA_HEAD_PERM)].reshape(d, SWA_HEADS * SWA_DH)
    parts = dict(ckv=ckv, cq=cq, kr=kr_wide, dk=dk, sk=sk, dq=dq, sq=sq, sv=sv, gates=gates)
    return jnp.concatenate([parts[name] for name in _COLS], axis=1).astype(BF16), dv.T.astype(BF16)


def _pad_heads(w, used):
    r = w.shape[0]
    w = w.reshape(r, MLA_HEADS, used)
    return jnp.pad(w, ((0, 0), (0, 0), (0, LANES - used))).reshape(r, MLA_HEADS * LANES)


def _bounded_tiles(bound_x, bound_c, sink):
    row = {name: i for i, name in enumerate(BOUND_ROWS)}
    bx, bc = bound_x[:, :, :, 0], bound_c[:, :, :, 0]
    keys = jnp.maximum(jnp.max(bx, axis=1), jnp.max(bc, axis=1))
    values_ok = keys[:, row['v_abs']] <= VALUE_BOUND
    sink_ok = jnp.max(jnp.abs(sink)) * LOG2E <= LOGIT_BOUND
    result = []
    for b, rows_per_sub in ((bx, SUB_ROWS), (bc, min(SUB_ROWS, bc.shape[1] * SUB_ROWS))):
        n = b.shape[1] * rows_per_sub
        per_mixer = {}
        for mixer, tile in (('mla', Q_TILE), ('diff', Q_TILE), ('swa', SWA_BLOCKS * WINDOW)):
            subs_per_tile = min(tile, n) // rows_per_sub
            q2 = jnp.max(b[:, :, row[mixer + '_q']].reshape(b.shape[0], -1, subs_per_tile), axis=2)
            ok = q2 * keys[:, None, row[mixer + '_k']] * BF16_NORM2_SLACK <= LOGIT_BOUND * LOGIT_BOUND
            ok = ok & values_ok[:, None] & (sink_ok if mixer == 'swa' else True)
            per_mixer[mixer] = ok.astype(jnp.int32).reshape(-1)
        result.append(per_mixer)
    return result


def kernel(x, c, ctx, c_ctx, w_ada, b_ada, attn_pre_norm, attn_post_norm, ffn_pre_norm, ffn_post_norm, w_in, mla_q_norm, w_uq, mla_kv_norm, w_ukv, diff_lambda, diff_norm, swa_sink, w_branch_a, w_branch_b, w_branch_c, w_o, w_gate_up, w_down):
    bsz, seq, d = x.shape
    depth = w_ada.shape[0]
    tabs = _rope_tables(seq)
    cond = jnp.concatenate([c, c_ctx[None, :], jnp.zeros((8 - bsz - 1, d), c.dtype)], axis=0)
    row = lambda v: v.reshape(1, -1)

    for l in range(depth):
        last = l == depth - 1
        lam_init = 0.8 - 0.6 * math.exp(-0.3 * l)
        mods = _adaln(cond, w_ada[l], b_ada[l])
        mx = mods[:bsz].reshape(bsz, N_MODS, d)
        mc = mods[bsz:bsz + 1].reshape(1, N_MODS, d)

        wcat, wdvt = _pack_in_proj(w_in[l])
        ukv = w_ukv[l].reshape(MLA_RANK, MLA_HEADS, MLA_NOPE + MLA_V)
        wuk = _pad_heads(ukv[:, :, :MLA_NOPE].reshape(MLA_RANK, -1), MLA_NOPE).astype(BF16)
        wuvt = ukv[:, :, MLA_NOPE:].reshape(MLA_RANK, MLA_HEADS * MLA_V).T.astype(BF16)
        wuq = _pad_heads(w_uq[l], MLA_NOPE + MLA_ROPE).astype(BF16)
        proj_w = (wcat, wdvt, row(mla_kv_norm[l]), wuk, wuvt, row(mla_q_norm[l]), wuq)
        sink_cols = jnp.repeat(swa_sink[l].reshape(SWA_KV_HEADS, 1, SWA_GROUP), WINDOW, axis=2)
        wa, wb = w_branch_a[l].astype(BF16), w_branch_b[l].astype(BF16)
        wc = w_branch_c[l].reshape(SWA_HEADS, SWA_DH, d)[np.array(SWA_HEAD_PERM)].reshape(-1, d).astype(BF16)
        wo, wgu, wd = w_o[l].astype(BF16), w_gate_up[l].astype(BF16), w_down[l].astype(BF16)
        dnorm = row(diff_norm[l])
        pre_a, post_a = row(attn_pre_norm[l]), row(attn_post_norm[l])
        pre_f, post_f = row(ffn_pre_norm[l]), row(ffn_post_norm[l])

        px = _in_proj(x, mx, pre_a, tabs, *proj_w, rope=True, with_q=True)
        pc = _in_proj(ctx, mc, pre_a, tabs, *proj_w, rope=False, with_q=not last)
        mkx, mvx, dkx, dvx, skx, svx, mqx, dqx, sqx, gx, bound_x = px
        mkc, mvc, dkc, dvc, skc, svc = pc[:6]
        bounded_x, bounded_c = _bounded_tiles(bound_x, pc[-1], swa_sink[l])
        o_a = _mla(bounded_x['mla'], mqx, [mkc, mkx], [mvc, mvx])
        o_b = _diff(bounded_x['diff'], dqx, diff_lambda[l], dnorm, [dkc, dkx], [dvc, dvx], lam_init=lam_init)
        o_c = _swa(bounded_x['swa'], sqx, sink_cols, skc, svc, skx, svx)
        x = _merge(x, mx, post_a, o_a, o_b, o_c, gx, wa, wb, wc, wo)
        if not last:
            mqc, dqc, sqc, gc = pc[6:-1]
            p_a = _mla(bounded_c['mla'], mqc, [mkc], [mvc])
            p_b = _diff(bounded_c['diff'], dqc, diff_lambda[l], dnorm, [dkc], [dvc], lam_init=lam_init)
            p_c = _swa(bounded_c['swa'], sqc, sink_cols, skc, svc)
            ctx = _merge(ctx, mc, post_a, p_a, p_b, p_c, gc, wa, wb, wc, wo)

        x = _ffn(x, mx, pre_f, post_f, wgu, wd)
        if not last:
            ctx = _ffn(ctx, mc, pre_f, post_f, wgu, wd)
    return x
```

```python
import functools
import math

import jax
import jax.numpy as jnp
import numpy as np
from jax import lax
from jax.experimental import pallas as pl
from jax.experimental.pallas import tpu as pltpu

D_MODEL = 1024
GRID_W = 64
EPS = 1e-6
ROPE_BASE = 10000.0
LOG2E = math.log2(math.e)

MLA_HEADS = 8
MLA_RANK = 256
MLA_NOPE = 64
MLA_ROPE = 32
MLA_V = 64
MLA_SCALE = (MLA_NOPE + MLA_ROPE) ** -0.5

DIFF_HEADS = 4
DIFF_DH = 64
DIFF_SCALE = DIFF_DH ** -0.5

SWA_HEADS = 8
SWA_KV_HEADS = 2
SWA_DH = 64
SWA_GROUP = SWA_HEADS // SWA_KV_HEADS
WINDOW = 128
SWA_SCALE = SWA_DH ** -0.5

D_FF = 2816
N_MODS = 6

LANES = 128
BF16_ROWS = 16
VMEM_LIMIT = 56 * 1024 * 1024

ROW_TILE = 512
FF_TILE = 1024
SUB_ROWS = 256
BOUND_ROWS = ('mla_q', 'diff_q', 'swa_q', 'mla_k', 'diff_k', 'swa_k', 'v_abs')
BF16_NORM2_SLACK = (1.0 + 2.0 ** -8) ** 4
Q_TILE = 1024
Q_SUB = 256
KEY_CHUNK = 1024
SWA_BLOCKS = 8
LOGIT_BOUND = 64.0
VALUE_BOUND = 2.0 ** 40

_COLS = {}
_off = 0
for _name, _width in (('ckv', 256), ('kr', 128), ('dk', 512), ('dv', 512), ('sk', 128), ('sv', 128),
                      ('cq', 256), ('dq', 512), ('sq', 512), ('gates', 3072)):
    _COLS[_name] = (_off, _off + _width)
    _off += _width
IN_COLS = _off
W_PIECES = (0,)


def _w_cols(w_refs, a, b):
    p = max(i for i, start in enumerate(W_PIECES) if start <= a)
    assert b <= (W_PIECES + (IN_COLS,))[p + 1]
    return w_refs[p][:, a - W_PIECES[p]:b - W_PIECES[p]]
SWA_HEAD_PERM = (0, 4, 1, 5, 2, 6, 3, 7)

BF16 = jnp.bfloat16
F32 = jnp.float32


def _dot(a, b):
    return jnp.dot(a, b, preferred_element_type=F32)


def _dot_nt(a, b):
    return lax.dot_general(a, b, (((1,), (1,)), ((), ())), preferred_element_type=F32)


def _rms(x, w):
    return x * lax.rsqrt(jnp.mean(x * x, axis=-1, keepdims=True) + EPS) * w


def _rope(x, cos, sin, first, shift):
    partner = jnp.where(first, pltpu.roll(x, LANES - shift, 1), pltpu.roll(x, shift, 1))
    return x * cos + partner * sin


def _const_spec(shape):
    nd = len(shape)
    return pl.BlockSpec(shape, lambda *_: (0,) * nd, pipeline_mode=pl.Buffered(1))


def _params(n_axes):
    return pltpu.CompilerParams(dimension_semantics=('arbitrary',) * n_axes,
                                vmem_limit_bytes=VMEM_LIMIT)


def _adaln_kernel(c_ref, w_ref, b_ref, o_ref):
    c = c_ref[...]
    s = c * jax.nn.sigmoid(c)
    o_ref[...] = _dot(s.astype(BF16), w_ref[...].astype(BF16)) + b_ref[...]


def _adaln(cond, w, b):
    rows, d = cond.shape
    n = w.shape[1]
    tn = n // 4
    return pl.pallas_call(
        _adaln_kernel,
        grid=(n // tn,),
        in_specs=[pl.BlockSpec((rows, d), lambda j: (0, 0)),
                  pl.BlockSpec((d, tn), lambda j: (0, j)),
                  pl.BlockSpec((1, tn), lambda j: (0, j))],
        out_specs=pl.BlockSpec((rows, tn), lambda j: (0, j)),
        out_shape=jax.ShapeDtypeStruct((rows, n), F32),
        compiler_params=_params(1),
        name='adaln',
    )(cond, w, b.reshape(1, n))


def _in_proj_kernel(x_ref, mod_ref, pre_ref, cos64_ref, sin64_ref, cosm_ref, sinm_ref, *refs, rope, with_q):
    n_w = len(W_PIECES)
    w_ref = refs[:n_w]
    kvn_ref, wuk_ref, wuv_ref, qn_ref, wuq_ref = refs[n_w:n_w + 5]
    outs = refs[n_w + 5:]
    tm = x_ref.shape[1]
    sub = min(SUB_ROWS, tm)
    phases = [_in_proj_rows(slice(r0, r0 + sub), x_ref, mod_ref, pre_ref, cos64_ref, sin64_ref, cosm_ref,
                            sinm_ref, w_ref, kvn_ref, wuk_ref, wuv_ref, qn_ref, wuq_ref, outs,
                            rope, with_q) for r0 in range(0, tm, sub)]
    next(phases[0])
    while phases:
        for rows in list(phases):
            try:
                next(rows)
            except StopIteration:
                phases.remove(rows)


def _chunk_norm2(z):
    return functools.reduce(jnp.maximum, [jnp.sum(z[:, i:i + LANES] * z[:, i:i + LANES], axis=1, keepdims=True)
                                          for i in range(0, z.shape[1], LANES)])


def _in_proj_rows(rs, x_ref, mod_ref, pre_ref, cos64_ref, sin64_ref, cosm_ref, sinm_ref,
                  w_ref, kvn_ref, wuk_ref, wuv_ref, qn_ref, wuq_ref, outs, rope, with_q):
    x = x_ref[0, rs]
    mod = mod_ref[0]
    h = _rms(x, pre_ref[...]) * (1.0 + mod[1:2]) + mod[0:1]
    hb = h.astype(BF16)
    yield

    def proj(name):
        return _dot(hb, _w_cols(w_ref, *_COLS[name]))

    lane = lax.broadcasted_iota(jnp.int32, (x.shape[0], LANES), 1)
    first64 = (lane & 31) < 16
    firstm = (lane & 15) < 8
    if rope:
        cos64, sin64, cosm, sinm = cos64_ref[rs], sin64_ref[rs], cosm_ref[rs], sinm_ref[rs]

    def rope64(z, scale):
        if not rope:
            return z if scale == 1.0 else z * scale
        c, s = (cos64, sin64) if scale == 1.0 else (cos64 * scale, sin64 * scale)
        return jnp.concatenate(
            [_rope(z[:, i:i + LANES], c, s, first64, 16) for i in range(0, z.shape[1], LANES)], axis=1)

    (mk_ref, mvt_ref, dk_ref, dvt_ref, sk_ref, sv_ref), q_outs, bound_ref = outs[:6], outs[6:-1], outs[-1]
    norms = {}

    ckv_raw = proj('ckv')
    if with_q:
        cq_raw = proj('cq')
    kr = proj('kr')
    dk = rope64(proj('dk'), 1.0)
    dk_ref[0, rs] = dk.astype(BF16)
    norms['diff_k'] = _chunk_norm2(dk)
    yield
    ckv = _rms(ckv_raw, kvn_ref[...]).astype(BF16)
    dvt = proj('dv').T
    dvt_ref[0, :, rs] = dvt.astype(BF16)
    yield
    k_nope = _dot(ckv, wuk_ref[...])
    if rope:
        kr = _rope(kr, cosm, sinm, firstm, 8)
    mk = jnp.concatenate([k_nope[:, i:i + LANES] + kr for i in range(0, MLA_HEADS * LANES, LANES)], axis=1)
    mk_ref[0, rs] = mk.astype(BF16)
    norms['mla_k'] = _chunk_norm2(mk)
    yield
    mvt = _dot(ckv, wuv_ref[...]).T
    mvt_ref[0, :, rs] = mvt.astype(BF16)
    sk = rope64(proj('sk'), 1.0)
    sk_ref[0, rs] = sk.astype(BF16)
    norms['swa_k'] = _chunk_norm2(sk)
    sv = proj('sv')
    sv_ref[0, rs] = sv.astype(BF16)
    yield

    if with_q:
        mq_ref, dq_ref, sq_ref, g_ref = q_outs
        cq = _rms(cq_raw, qn_ref[...]).astype(BF16)
        dq = rope64(proj('dq'), DIFF_SCALE * LOG2E)
        dq_ref[0, rs] = dq.astype(BF16)
        norms['diff_q'] = _chunk_norm2(dq)
        yield
        q = _dot(cq, wuq_ref[...])
        mla_scale = MLA_SCALE * LOG2E
        mq = []
        for i in range(0, MLA_HEADS * LANES, LANES):
            qi = q[:, i:i + LANES]
            mq.append(_rope(qi, cosm * mla_scale, sinm * mla_scale, firstm, 8) if rope else qi * mla_scale)
        mq = jnp.concatenate(mq, axis=1)
        mq_ref[0, rs] = mq.astype(BF16)
        norms['mla_q'] = _chunk_norm2(mq)
        yield
        sq = rope64(proj('sq'), SWA_SCALE * LOG2E)
        sq_ref[0, rs] = sq.astype(BF16)
        norms['swa_q'] = _chunk_norm2(sq)
        a, b = _COLS['gates']
        for i in range(a, b, 512):
            yield
            g_ref[0, rs, i - a:i - a + 512] = jax.nn.sigmoid(_dot(hb, _w_cols(w_ref, i, i + 512))).astype(BF16)

    vmax = functools.reduce(jnp.maximum, [jnp.max(jnp.max(jnp.abs(v), axis=1, keepdims=True), axis=0, keepdims=True)
                                          for v in (mvt, dvt, sv)])
    bounds = [jnp.max(norms[name], axis=0, keepdims=True) if name in norms else jnp.zeros((1, 1), F32)
              for name in BOUND_ROWS[:6]] + [vmax, jnp.zeros((1, 1), F32)]
    bound_ref[0, rs.start // (rs.stop - rs.start)] = jnp.broadcast_to(jnp.concatenate(bounds, axis=0), (8, LANES))


def _in_proj(x, mods, pre_w, tabs, w_pieces, kvn, wuk, wuv, qn, wuq, *, rope, with_q):
    bsz, n, d = x.shape
    tm = min(ROW_TILE, n)
    sub = min(SUB_ROWS, tm)
    per_batch_mod = mods.shape[0] > 1
    outs = [(MLA_HEADS * LANES, False), (MLA_HEADS * MLA_V, True), (512, False), (512, True),
            (128, False), (128, False)]
    if with_q:
        outs += [(MLA_HEADS * LANES, False), (512, False), (512, False), (3 * D_MODEL, False)]
    row_spec = lambda w: pl.BlockSpec((1, tm, w), lambda b, i: (b, i, 0))
    col_spec = lambda w: pl.BlockSpec((1, w, tm), lambda b, i: (b, 0, i))
    tab_spec = pl.BlockSpec((tm, LANES), lambda b, i: (i, 0))
    kern = functools.partial(_in_proj_kernel, rope=rope, with_q=with_q)
    return pl.pallas_call(
        kern,
        grid=(bsz, n // tm),
        in_specs=[row_spec(d),
                  pl.BlockSpec((1, N_MODS, d), (lambda b, i: (b, 0, 0)) if per_batch_mod else (lambda b, i: (0, 0, 0))),
                  _const_spec((1, d)),
                  tab_spec, tab_spec, tab_spec, tab_spec,
                  *[_const_spec(w.shape) for w in w_pieces], _const_spec(kvn.shape),
                  _const_spec(wuk.shape), _const_spec(wuv.shape), _const_spec(qn.shape), _const_spec(wuq.shape)],
        out_specs=[col_spec(w) if t else row_spec(w) for w, t in outs]
                  + [pl.BlockSpec((1, tm // sub, 8, LANES), lambda b, i: (b, i, 0, 0))],
        out_shape=[jax.ShapeDtypeStruct((bsz, w, n) if t else (bsz, n, w), BF16) for w, t in outs]
                  + [jax.ShapeDtypeStruct((bsz, n // sub, 8, LANES), F32)],
        compiler_params=_params(2),
        name='in_proj_rope' if rope else 'in_proj_ctx',
    )(x, mods, pre_w, *tabs, *w_pieces, kvn, wuk, wuv, qn, wuq)


def _kv_chunks(k_refs, vt_refs, col, vrow, dv):
    chunks = []
    for k_ref, vt_ref in zip(k_refs, vt_refs):
        n = k_ref.shape[1]
        step = min(KEY_CHUNK, n)
        for c in range(0, n, step):
            chunks.append((functools.partial(lambda r, c, s: r[0, c:c + s, col:col + LANES], k_ref, c, step),
                           functools.partial(lambda r, c, s: r[0, vrow:vrow + dv, c:c + s], vt_ref, c, step)))
    return chunks


def _flash_chains(chains):
    n = len(chains[0][1])
    state = [(None, None)] * len(chains)
    s_next = [_dot_nt(chunks[0][0](), q) for q, chunks in chains]
    for i in range(n):
        for ci, (q, chunks) in enumerate(chains):
            m, acc = state[ci]
            s = s_next[ci]
            if i + 1 < n:
                s_next[ci] = _dot_nt(chunks[i + 1][0](), q)
            mc = jnp.max(s, axis=0, keepdims=True)
            m_new = mc if m is None else jnp.maximum(m, mc)
            pv = _dot(_with_ones_rows(chunks[i][1]()), jnp.exp2(s - m_new).astype(BF16))
            acc = pv if m is None else jnp.exp2(m - m_new) * acc + pv
            state[ci] = (m_new, acc)
    return [acc for _, acc in state]


def _unshifted_chains(chains):
    n = len(chains[0][1])
    acc = [None] * len(chains)
    s_next = [_dot_nt(chunks[0][0](), q) for q, chunks in chains]
    for i in range(n):
        for ci, (q, chunks) in enumerate(chains):
            s = s_next[ci]
            if i + 1 < n:
                s_next[ci] = _dot_nt(chunks[i + 1][0](), q)
            pv = _dot(_with_ones_rows(chunks[i][1]()), jnp.exp2(s).astype(BF16))
            acc[ci] = pv if acc[ci] is None else acc[ci] + pv
    return acc


def _with_ones_rows(vt):
    return jnp.concatenate([vt, jnp.ones((BF16_ROWS, vt.shape[1]), BF16)], axis=0)


def _run_chains(chains, bounded, finish):
    @pl.when(bounded)
    def _():
        finish(_unshifted_chains(chains))

    @pl.when(jnp.logical_not(bounded))
    def _():
        finish(_flash_chains(chains))


def _tile_bounded(bounded_ref, tile_axis):
    return bounded_ref[pl.program_id(0) * pl.num_programs(tile_axis) + pl.program_id(tile_axis)] != 0


def _prefetch_grid(grid, in_specs, out_specs):
    return pltpu.PrefetchScalarGridSpec(num_scalar_prefetch=1, grid=grid, in_specs=in_specs, out_specs=out_specs)


def _mla_kernel(bounded_ref, *refs, n_seg):
    q_ref = refs[0]
    k_refs = refs[1:1 + n_seg]
    vt_refs = refs[1 + n_seg:1 + 2 * n_seg]
    o_ref = refs[1 + 2 * n_seg]
    tq = q_ref.shape[1]
    subs = range(0, tq, Q_SUB)
    chains = []
    for j in range(2):
        chunks = _kv_chunks(k_refs, vt_refs, j * LANES, j * MLA_V, MLA_V)
        chains += [(q_ref[0, r:r + Q_SUB, j * LANES:(j + 1) * LANES], chunks) for r in subs]

    def finish(res):
        for ri, r in enumerate(subs):
            pair = [res[j * len(subs) + ri] for j in range(2)]
            o_t = jnp.concatenate([a[:MLA_V] / a[MLA_V:MLA_V + 1] for a in pair], axis=0)
            o_ref[0, r:r + Q_SUB, :] = o_t.T.astype(BF16)

    _run_chains(chains, _tile_bounded(bounded_ref, 2), finish)


def _mla(bounded, q, ks, vts):
    bsz, n, _ = q.shape
    tq = min(Q_TILE, n)
    n_seg = len(ks)
    return pl.pallas_call(
        functools.partial(_mla_kernel, n_seg=n_seg),
        grid_spec=_prefetch_grid(
            (bsz, MLA_HEADS // 2, n // tq),
            [pl.BlockSpec((1, tq, 2 * LANES), lambda b, h, i, _: (b, i, h))]
            + [pl.BlockSpec((1, k.shape[1], 2 * LANES), lambda b, h, i, _: (b, 0, h)) for k in ks]
            + [pl.BlockSpec((1, 2 * MLA_V, v.shape[2]), lambda b, h, i, _: (b, h, 0)) for v in vts],
            pl.BlockSpec((1, tq, LANES), lambda b, h, i, _: (b, i, h))),
        out_shape=jax.ShapeDtypeStruct((bsz, n, MLA_HEADS * MLA_V), BF16),
        compiler_params=_params(3),
        name=f'mla_attn_{n_seg}seg',
    )(bounded, q, *ks, *vts)


def _diff_kernel(bounded_ref, *refs, n_seg, lam_init):
    q_ref, lam_ref, nrm_ref = refs[:3]
    k_refs = refs[3:3 + n_seg]
    vt_refs = refs[3 + n_seg:3 + 2 * n_seg]
    o_ref = refs[3 + 2 * n_seg]
    lv = lam_ref[...]
    lam = (jnp.exp(jnp.sum(lv[0:1] * lv[1:2], axis=-1, keepdims=True))
           - jnp.exp(jnp.sum(lv[2:3] * lv[3:4], axis=-1, keepdims=True)) + lam_init)
    dv = 2 * DIFF_DH
    tq = q_ref.shape[1]
    subs = range(0, tq, Q_SUB)
    lane = lax.broadcasted_iota(jnp.int32, (Q_SUB, LANES), 1)
    zero = jnp.zeros((Q_SUB, LANES), BF16)
    chunks = _kv_chunks(k_refs, vt_refs, 0, 0, dv)
    chains = []
    for comp in range(2):
        keep = (lane < DIFF_DH) if comp == 0 else (lane >= DIFF_DH)
        chains += [(jnp.where(keep, q_ref[0, r:r + Q_SUB, :], zero), chunks) for r in subs]

    def finish(res):
        for ri, r in enumerate(subs):
            a0, a1 = res[ri], res[len(subs) + ri]
            o_t = a0[:dv] / a0[dv:dv + 1] - lam * (a1[:dv] / a1[dv:dv + 1])
            o_ref[0, r:r + Q_SUB, :] = (_rms(o_t.T, nrm_ref[...]) * (1.0 - lam_init)).astype(BF16)

    _run_chains(chains, _tile_bounded(bounded_ref, 2), finish)


def _diff(bounded, q, lam_w, nrm_w, ks, vts, *, lam_init):
    bsz, n, _ = q.shape
    tq = min(Q_TILE, n)
    n_seg = len(ks)
    const = lambda a: pl.BlockSpec(a.shape, lambda b, h, i, _: (0, 0), pipeline_mode=pl.Buffered(1))
    return pl.pallas_call(
        functools.partial(_diff_kernel, n_seg=n_seg, lam_init=lam_init),
        grid_spec=_prefetch_grid(
            (bsz, DIFF_HEADS, n // tq),
            [pl.BlockSpec((1, tq, LANES), lambda b, h, i, _: (b, i, h)), const(lam_w), const(nrm_w)]
            + [pl.BlockSpec((1, k.shape[1], LANES), lambda b, h, i, _: (b, 0, h)) for k in ks]
            + [pl.BlockSpec((1, 2 * DIFF_DH, v.shape[2]), lambda b, h, i, _: (b, h, 0)) for v in vts],
            pl.BlockSpec((1, tq, LANES), lambda b, h, i, _: (b, i, h))),
        out_shape=jax.ShapeDtypeStruct((bsz, n, DIFF_HEADS * 2 * DIFF_DH), BF16),
        compiler_params=_params(3),
        name=f'diff_attn_{n_seg}seg',
    )(bounded, q, lam_w, nrm_w, *ks, *vts)


def _swa_kernel(bounded_ref, *refs, local, seq):
    if local:
        q_ref, sink_ref, kc_ref, vc_ref, kx_ref, vx_ref, o_ref = refs
    else:
        q_ref, sink_ref, kc_ref, vc_ref, o_ref = refs
    blk = WINDOW
    n_blocks = q_ref.shape[1] // blk
    nb = seq // blk
    cols = SWA_GROUP * blk
    g = pl.program_id(1)
    sinks = [sink_ref[half] * LOG2E for half in range(SWA_KV_HEADS)]
    bounded = _tile_bounded(bounded_ref, 1)

    lane = lax.broadcasted_iota(jnp.int32, (blk, LANES), 1)
    kc = kc_ref[0]
    vct = vc_ref[0].T
    if local:
        key_r = lax.broadcasted_iota(jnp.int32, (blk, cols), 0)
        qry_r = lax.broadcasted_iota(jnp.int32, (blk, cols), 1) & (blk - 1)

        def block_rows(ref, idx):
            return ref[0, pl.ds(pl.multiple_of(idx * blk, blk), blk), :]

        first = g * n_blocks
        win = [jnp.clip(first - 1 + t, 0, nb - 1) for t in range(n_blocks + 2)]
        k_blocks = [block_rows(kx_ref, w) for w in win]
        vt_blocks = [block_rows(vx_ref, w).T for w in win]

    def run(shifted):
        pairs = [(b, half) for b in range(n_blocks) for half in range(SWA_KV_HEADS)]
        window = {}
        if local:
            for b in range(n_blocks):
                i = first + b
                window[b] = (jnp.logical_and(qry_r <= key_r, i > 0), jnp.logical_and(key_r <= qry_r, i < nb - 1),
                             jnp.concatenate(k_blocks[b:b + 3], axis=0),
                             jnp.concatenate(vt_blocks[b:b + 3], axis=1))
        scores = {}
        for b, half in pairs:
            keep = (lane < SWA_DH) if half == 0 else (lane >= SWA_DH)
            q = jnp.concatenate(
                [jnp.where(keep, q_ref[0, b * blk:(b + 1) * blk, c * LANES:(c + 1) * LANES],
                           jnp.zeros((blk, LANES), BF16)) for c in range(SWA_GROUP)], axis=0)
            scores[b, half] = (_dot_nt(kc, q), _dot_nt(window[b][2], q) if local else None)
        probs = {}
        for b, half in pairs:
            s_c, s_l = scores[b, half]
            sink = sinks[half]
            if local:
                prev_vis, next_vis = window[b][:2]
            if shifted:
                m = jnp.maximum(jnp.max(s_c, axis=0, keepdims=True), sink)
                if local:
                    neg = -jnp.inf
                    m = jnp.maximum(m, jnp.max(jnp.where(prev_vis, s_l[:blk], neg), axis=0, keepdims=True))
                    m = jnp.maximum(m, jnp.max(s_l[blk:2 * blk], axis=0, keepdims=True))
                    m = jnp.maximum(m, jnp.max(jnp.where(next_vis, s_l[2 * blk:], neg), axis=0, keepdims=True))
                    s_l = s_l - m
                s_c, sink = s_c - m, sink - m
            p_l = None
            if local:
                e_l = jnp.exp2(s_l)
                p_l = jnp.concatenate([jnp.where(prev_vis, e_l[:blk], 0.0), e_l[blk:2 * blk],
                                       jnp.where(next_vis, e_l[2 * blk:], 0.0)], axis=0).astype(BF16)
            probs[b, half] = (jnp.exp2(s_c).astype(BF16), p_l, jnp.exp2(sink))
        outs = {}
        for b, half in pairs:
            p_c, p_l, e_sink = probs[b, half]
            rows = slice(half * SWA_DH, (half + 1) * SWA_DH)
            acc = _dot(_with_ones_rows(vct[rows]), p_c)
            if local:
                acc = acc + _dot(_with_ones_rows(window[b][3][rows]), p_l)
            outs[b, half] = acc[:SWA_DH] / (acc[SWA_DH:SWA_DH + 1] + e_sink)
        for b in range(n_blocks):
            o_t = jnp.concatenate([outs[b, half] for half in range(SWA_KV_HEADS)], axis=0)
            for c in range(SWA_GROUP):
                o_ref[0, b * blk:(b + 1) * blk, c * LANES:(c + 1) * LANES] = \
                    o_t[:, c * blk:(c + 1) * blk].T.astype(BF16)

    @pl.when(bounded)
    def _():
        run(False)

    @pl.when(jnp.logical_not(bounded))
    def _():
        run(True)


def _swa(bounded, q, sink_cols, kc, vc, kx=None, vx=None):
    bsz, n, _ = q.shape
    local = kx is not None
    rows = min(SWA_BLOCKS * WINDOW, n)
    full = lambda a: pl.BlockSpec((1,) + a.shape[1:], lambda b, i, _: (b, 0, 0))
    kv = [kc, vc] + ([kx, vx] if local else [])
    return pl.pallas_call(
        functools.partial(_swa_kernel, local=local, seq=n),
        grid_spec=_prefetch_grid(
            (bsz, n // rows),
            [pl.BlockSpec((1, rows, SWA_HEADS * SWA_DH), lambda b, i, _: (b, i, 0)),
             pl.BlockSpec(sink_cols.shape, lambda b, i, _: (0, 0, 0), pipeline_mode=pl.Buffered(1))]
            + [full(a) for a in kv],
            pl.BlockSpec((1, rows, SWA_HEADS * SWA_DH), lambda b, i, _: (b, i, 0))),
        out_shape=jax.ShapeDtypeStruct((bsz, n, SWA_HEADS * SWA_DH), BF16),
        compiler_params=_params(2),
        name='swa_attn_local' if local else 'swa_attn_ctx',
    )(bounded, q, sink_cols, *kv)


def _merge_kernel(x_ref, mod_ref, post_ref, oa_ref, ob_ref, oc_ref, g_ref,
                  wa_ref, wb_ref, wc_ref, wo_ref, o_ref):
    d = D_MODEL
    y = (g_ref[0, :, 0:d].astype(F32) * _dot(oa_ref[0], wa_ref[...])
         + g_ref[0, :, d:2 * d].astype(F32) * _dot(ob_ref[0], wb_ref[...])
         + g_ref[0, :, 2 * d:3 * d].astype(F32) * _dot(oc_ref[0], wc_ref[...]))
    y = _dot(y.astype(BF16), wo_ref[...])
    mod = mod_ref[0]
    o_ref[0] = x_ref[0] + mod[2:3] * _rms(y, post_ref[...])


def _merge(x, mods, post_w, oa, ob, oc, gates, wa, wb, wc, wo):
    bsz, n, d = x.shape
    tm = min(FF_TILE, n)
    per_batch_mod = mods.shape[0] > 1
    row_spec = lambda w: pl.BlockSpec((1, tm, w), lambda b, i: (b, i, 0))
    return pl.pallas_call(
        _merge_kernel,
        grid=(bsz, n // tm),
        in_specs=[row_spec(d),
                  pl.BlockSpec((1, N_MODS, d), (lambda b, i: (b, 0, 0)) if per_batch_mod else (lambda b, i: (0, 0, 0))),
                  _const_spec((1, d)),
                  row_spec(512), row_spec(512), row_spec(512), row_spec(3 * d),
                  _const_spec(wa.shape), _const_spec(wb.shape), _const_spec(wc.shape), _const_spec(wo.shape)],
        out_specs=row_spec(d),
        out_shape=jax.ShapeDtypeStruct((bsz, n, d), F32),
        compiler_params=_params(2),
        name='merge_out_proj',
    )(x, mods, post_w, oa, ob, oc, gates, wa, wb, wc, wo)


def _ffn_kernel(x_ref, mod_ref, pre_ref, post_ref, wgu_ref, wd_ref, o_ref):
    tm = x_ref.shape[1]
    sub = min(SUB_ROWS, tm)
    mod = mod_ref[0]
    for r0 in range(0, tm, sub):
        rs = slice(r0, r0 + sub)
        x = x_ref[0, rs]
        h = (_rms(x, pre_ref[...]) * (1.0 + mod[4:5]) + mod[3:4]).astype(BF16)
        g = _dot(h, wgu_ref[:, 0:D_FF])
        u = _dot(h, wgu_ref[:, D_FF:2 * D_FF])
        act = (g * jax.nn.sigmoid(g) * u).astype(BF16)
        f = _dot(act, wd_ref[...])
        o_ref[0, rs] = x + mod[5:6] * _rms(f, post_ref[...])


def _ffn(x, mods, pre_w, post_w, wgu, wd):
    bsz, n, d = x.shape
    tm = min(FF_TILE, n)
    per_batch_mod = mods.shape[0] > 1
    row_spec = pl.BlockSpec((1, tm, d), lambda b, i: (b, i, 0))
    return pl.pallas_call(
        _ffn_kernel,
        grid=(bsz, n // tm),
        in_specs=[row_spec,
                  pl.BlockSpec((1, N_MODS, d), (lambda b, i: (b, 0, 0)) if per_batch_mod else (lambda b, i: (0, 0, 0))),
                  _const_spec((1, d)), _const_spec((1, d)),
                  _const_spec(wgu.shape), _const_spec(wd.shape)],
        out_specs=row_spec,
        out_shape=jax.ShapeDtypeStruct((bsz, n, d), F32),
        compiler_params=_params(2),
        name='swiglu',
    )(x, mods, pre_w, post_w, wgu, wd)


def _rope_tables(seq):
    t = np.arange(seq)
    pos = np.stack([t // GRID_W, t % GRID_W]).astype(np.float64)
    lanes = np.arange(LANES)

    def table(rot_dim, slot):
        axis_dim = rot_dim // 2
        half = axis_dim // 2
        inv = ROPE_BASE ** (-np.arange(0, axis_dim, 2, dtype=np.float64) / axis_dim)
        valid = slot >= 0
        axis, j = np.divmod(np.where(valid, slot, 0), axis_dim)
        ang = pos[axis].T * inv[j % half][None, :]
        sign = np.where(j < half, -1.0, 1.0)
        cos = np.where(valid[None, :], np.cos(ang), 1.0)
        sin = np.where(valid[None, :], np.sin(ang) * sign[None, :], 0.0)
        return jnp.asarray(cos, F32), jnp.asarray(sin, F32)

    cos64, sin64 = table(SWA_DH, lanes % SWA_DH)
    in_rope = (lanes >= MLA_NOPE) & (lanes < MLA_NOPE + MLA_ROPE)
    cosm, sinm = table(MLA_ROPE, np.where(in_rope, lanes - MLA_NOPE, -1))
    return cos64, sin64, cosm, sinm


def _pack_in_proj(w):
    d = w.shape[0]
    kr0 = MLA_RANK
    mid0 = kr0 + MLA_ROPE
    sq0 = mid0 + (_COLS['sq'][0] - _COLS['dk'][0])
    gates0 = sq0 + SWA_HEADS * SWA_DH
    kr_wide = jnp.pad(w[:, kr0:mid0], ((0, 0), (MLA_NOPE, LANES - MLA_NOPE - MLA_ROPE)))
    sq = w[:, sq0:gates0].reshape(d, SWA_HEADS, SWA_DH)[:, np.array(SWA_HEAD_PERM)].reshape(d, SWA_HEADS * SWA_DH)
    pieces = (w[:, :kr0], kr_wide, w[:, mid0:sq0], sq, w[:, gates0:])
    wcat = jnp.concatenate(pieces, axis=1)
    assert wcat.shape[1] == IN_COLS
    return (wcat.astype(BF16),)


def _pad_heads(w, used):
    r = w.shape[0]
    w = w.reshape(r, MLA_HEADS, used)
    return jnp.pad(w, ((0, 0), (0, 0), (0, LANES - used))).reshape(r, MLA_HEADS * LANES)


def _bounded_tiles(bound_x, bound_c, sink):
    row = {name: i for i, name in enumerate(BOUND_ROWS)}
    bx, bc = bound_x[:, :, :, 0], bound_c[:, :, :, 0]
    keys = jnp.maximum(jnp.max(bx, axis=1), jnp.max(bc, axis=1))
    values_ok = keys[:, row['v_abs']] <= VALUE_BOUND
    sink_ok = jnp.max(jnp.abs(sink)) * LOG2E <= LOGIT_BOUND
    result = []
    for b, rows_per_sub in ((bx, SUB_ROWS), (bc, min(SUB_ROWS, bc.shape[1] * SUB_ROWS))):
        n = b.shape[1] * rows_per_sub
        per_mixer = {}
        for mixer, tile in (('mla', Q_TILE), ('diff', Q_TILE), ('swa', SWA_BLOCKS * WINDOW)):
            subs_per_tile = min(tile, n) // rows_per_sub
            q2 = jnp.max(b[:, :, row[mixer + '_q']].reshape(b.shape[0], -1, subs_per_tile), axis=2)
            ok = q2 * keys[:, None, row[mixer + '_k']] * BF16_NORM2_SLACK <= LOGIT_BOUND * LOGIT_BOUND
            ok = ok & values_ok[:, None] & (sink_ok if mixer == 'swa' else True)
            per_mixer[mixer] = ok.astype(jnp.int32).reshape(-1)
        result.append(per_mixer)
    return result


def kernel(x, c, ctx, c_ctx, w_ada, b_ada, attn_pre_norm, attn_post_norm, ffn_pre_norm, ffn_post_norm, w_in, mla_q_norm, w_uq, mla_kv_norm, w_ukv, diff_lambda, diff_norm, swa_sink, w_branch_a, w_branch_b, w_branch_c, w_o, w_gate_up, w_down):
    bsz, seq, d = x.shape
    depth = w_ada.shape[0]
    tabs = _rope_tables(seq)
    cond = jnp.concatenate([c, c_ctx[None, :], jnp.zeros((8 - bsz - 1, d), c.dtype)], axis=0)
    row = lambda v: v.reshape(1, -1)

    for l in range(depth):
        last = l == depth - 1
        lam_init = 0.8 - 0.6 * math.exp(-0.3 * l)
        mods = _adaln(cond, w_ada[l], b_ada[l])
        mx = mods[:bsz].reshape(bsz, N_MODS, d)
        mc = mods[bsz:bsz + 1].reshape(1, N_MODS, d)

        w_pieces = _pack_in_proj(w_in[l])
        ukv = w_ukv[l].reshape(MLA_RANK, MLA_HEADS, MLA_NOPE + MLA_V)
        wuk = _pad_heads(ukv[:, :, :MLA_NOPE].reshape(MLA_RANK, -1), MLA_NOPE).astype(BF16)
        wuv = ukv[:, :, MLA_NOPE:].reshape(MLA_RANK, MLA_HEADS * MLA_V).astype(BF16)
        wuq = _pad_heads(w_uq[l], MLA_NOPE + MLA_ROPE).astype(BF16)
        proj_w = (w_pieces, row(mla_kv_norm[l]), wuk, wuv, row(mla_q_norm[l]), wuq)
        sink_cols = jnp.repeat(swa_sink[l].reshape(SWA_KV_HEADS, 1, SWA_GROUP), WINDOW, axis=2)
        wa, wb = w_branch_a[l].astype(BF16), w_branch_b[l].astype(BF16)
        wc = w_branch_c[l].reshape(SWA_HEADS, SWA_DH, d)[np.array(SWA_HEAD_PERM)].reshape(-1, d).astype(BF16)
        wo, wgu, wd = w_o[l].astype(BF16), w_gate_up[l].astype(BF16), w_down[l].astype(BF16)
        dnorm = row(diff_norm[l])
        pre_a, post_a = row(attn_pre_norm[l]), row(attn_post_norm[l])
        pre_f, post_f = row(ffn_pre_norm[l]), row(ffn_post_norm[l])

        px = _in_proj(x, mx, pre_a, tabs, *proj_w, rope=True, with_q=True)
        pc = _in_proj(ctx, mc, pre_a, tabs, *proj_w, rope=False, with_q=not last)
        mkx, mvx, dkx, dvx, skx, svx, mqx, dqx, sqx, gx, bound_x = px
        mkc, mvc, dkc, dvc, skc, svc = pc[:6]
        bounded_x, bounded_c = _bounded_tiles(bound_x, pc[-1], swa_sink[l])
        o_a = _mla(bounded_x['mla'], mqx, [mkc, mkx], [mvc, mvx])
        o_b = _diff(bounded_x['diff'], dqx, diff_lambda[l], dnorm, [dkc, dkx], [dvc, dvx], lam_init=lam_init)
        o_c = _swa(bounded_x['swa'], sqx, sink_cols, skc, svc, skx, svx)
        x = _merge(x, mx, post_a, o_a, o_b, o_c, gx, wa, wb, wc, wo)
        if not last:
            mqc, dqc, sqc, gc = pc[6:-1]
            p_a = _mla(bounded_c['mla'], mqc, [mkc], [mvc])
            p_b = _diff(bounded_c['diff'], dqc, diff_lambda[l], dnorm, [dkc], [dvc], lam_init=lam_init)
            p_c = _swa(bounded_c['swa'], sqc, sink_cols, skc, svc)
            ctx = _merge(ctx, mc, post_a, p_a, p_b, p_c, gc, wa, wb, wc, wo)

        x = _ffn(x, mx, pre_f, post_f, wgu, wd)
        if not last:
            ctx = _ffn(ctx, mc, pre_f, post_f, wgu, wd)
    return x
```

```python
import functools
import math

import jax
import jax.numpy as jnp
import numpy as np
from jax import lax
from jax.experimental import pallas as pl
from jax.experimental.pallas import tpu as pltpu

D_MODEL = 1024
GRID_W = 64
EPS = 1e-6
ROPE_BASE = 10000.0
LOG2E = math.log2(math.e)

MLA_HEADS = 8
MLA_RANK = 256
MLA_NOPE = 64
MLA_ROPE = 32
MLA_V = 64
MLA_SCALE = (MLA_NOPE + MLA_ROPE) ** -0.5

DIFF_HEADS = 4
DIFF_DH = 64
DIFF_SCALE = DIFF_DH ** -0.5

SWA_HEADS = 8
SWA_KV_HEADS = 2
SWA_DH = 64
SWA_GROUP = SWA_HEADS // SWA_KV_HEADS
WINDOW = 128
SWA_SCALE = SWA_DH ** -0.5

D_FF = 2816
N_MODS = 6

LANES = 128
BF16_ROWS = 16
VMEM_LIMIT = 56 * 1024 * 1024

ROW_TILE = 512
FF_TILE = 1024
SUB_ROWS = 256
BOUND_ROWS = ('mla_q', 'diff_q', 'swa_q', 'mla_k', 'diff_k', 'swa_k', 'v_abs')
BF16_NORM2_SLACK = (1.0 + 2.0 ** -8) ** 4
Q_TILE = 1024
Q_SUB = 256
KEY_CHUNK = 1024
SWA_BLOCKS = 8
LOGIT_BOUND = 64.0
VALUE_BOUND = 2.0 ** 40

_COLS = {}
_off = 0
for _name, _width in (('ckv', 256), ('kr', 128), ('dk', 512), ('dv', 512), ('sk', 128), ('sv', 128),
                      ('cq', 256), ('dq', 512), ('sq', 512), ('gates', 3072)):
    _COLS[_name] = (_off, _off + _width)
    _off += _width
IN_COLS = _off
W_PIECES = (0,)


def _w_cols(w_refs, a, b):
    p = max(i for i, start in enumerate(W_PIECES) if start <= a)
    assert b <= (W_PIECES + (IN_COLS,))[p + 1]
    return w_refs[p][:, a - W_PIECES[p]:b - W_PIECES[p]]
SWA_HEAD_PERM = (0, 4, 1, 5, 2, 6, 3, 7)

BF16 = jnp.bfloat16
F32 = jnp.float32


def _dot(a, b):
    return jnp.dot(a, b, preferred_element_type=F32)


def _dot_nt(a, b):
    return lax.dot_general(a, b, (((1,), (1,)), ((), ())), preferred_element_type=F32)


def _rms(x, w):
    return x * lax.rsqrt(jnp.mean(x * x, axis=-1, keepdims=True) + EPS) * w


def _rope(x, cos, sin, first, shift):
    partner = jnp.where(first, pltpu.roll(x, LANES - shift, 1), pltpu.roll(x, shift, 1))
    return x * cos + partner * sin


def _const_spec(shape):
    nd = len(shape)
    return pl.BlockSpec(shape, lambda *_: (0,) * nd, pipeline_mode=pl.Buffered(1))


def _params(n_axes):
    return pltpu.CompilerParams(dimension_semantics=('arbitrary',) * n_axes,
                                vmem_limit_bytes=VMEM_LIMIT)


def _adaln_kernel(c_ref, w_ref, b_ref, o_ref):
    c = c_ref[...]
    s = c * jax.nn.sigmoid(c)
    o_ref[...] = _dot(s.astype(BF16), w_ref[...].astype(BF16)) + b_ref[...]


def _adaln(cond, w, b):
    rows, d = cond.shape
    depth, _, n = w.shape
    tn = n // 4
    return pl.pallas_call(
        _adaln_kernel,
        grid=(depth, n // tn),
        in_specs=[pl.BlockSpec((rows, d), lambda l, j: (0, 0)),
                  pl.BlockSpec((None, d, tn), lambda l, j: (l, 0, j)),
                  pl.BlockSpec((None, 1, tn), lambda l, j: (l, 0, j))],
        out_specs=pl.BlockSpec((None, rows, tn), lambda l, j: (l, 0, j)),
        out_shape=jax.ShapeDtypeStruct((depth, rows, n), F32),
        compiler_params=_params(2),
        name='adaln',
    )(cond, w, b.reshape(depth, 1, n))


def _in_proj_kernel(x_ref, mod_ref, pre_ref, cos64_ref, sin64_ref, cosm_ref, sinm_ref, *refs, rope, with_q):
    n_w = len(W_PIECES)
    w_ref = refs[:n_w]
    kvn_ref, wuk_ref, wuv_ref, qn_ref, wuq_ref = refs[n_w:n_w + 5]
    outs = refs[n_w + 5:]
    tm = x_ref.shape[1]
    sub = min(SUB_ROWS, tm)
    phases = [_in_proj_rows(slice(r0, r0 + sub), x_ref, mod_ref, pre_ref, cos64_ref, sin64_ref, cosm_ref,
                            sinm_ref, w_ref, kvn_ref, wuk_ref, wuv_ref, qn_ref, wuq_ref, outs,
                            rope, with_q) for r0 in range(0, tm, sub)]
    next(phases[0])
    while phases:
        for rows in list(phases):
            try:
                next(rows)
            except StopIteration:
                phases.remove(rows)


def _chunk_norm2(z):
    return functools.reduce(jnp.maximum, [jnp.sum(z[:, i:i + LANES] * z[:, i:i + LANES], axis=1, keepdims=True)
                                          for i in range(0, z.shape[1], LANES)])


def _in_proj_rows(rs, x_ref, mod_ref, pre_ref, cos64_ref, sin64_ref, cosm_ref, sinm_ref,
                  w_ref, kvn_ref, wuk_ref, wuv_ref, qn_ref, wuq_ref, outs, rope, with_q):
    x = x_ref[0, rs]
    mod = mod_ref[0]
    h = _rms(x, pre_ref[...]) * (1.0 + mod[1:2]) + mod[0:1]
    hb = h.astype(BF16)
    yield

    def proj(name):
        return _dot(hb, _w_cols(w_ref, *_COLS[name]))

    lane = lax.broadcasted_iota(jnp.int32, (x.shape[0], LANES), 1)
    first64 = (lane & 31) < 16
    firstm = (lane & 15) < 8
    if rope:
        cos64, sin64, cosm, sinm = cos64_ref[rs], sin64_ref[rs], cosm_ref[rs], sinm_ref[rs]

    def rope64(z, scale):
        if not rope:
            return z if scale == 1.0 else z * scale
        c, s = (cos64, sin64) if scale == 1.0 else (cos64 * scale, sin64 * scale)
        return jnp.concatenate(
            [_rope(z[:, i:i + LANES], c, s, first64, 16) for i in range(0, z.shape[1], LANES)], axis=1)

    (mk_ref, mvt_ref, dk_ref, dvt_ref, sk_ref, sv_ref), q_outs, bound_ref = outs[:6], outs[6:-1], outs[-1]
    norms = {}

    ckv_raw = proj('ckv')
    if with_q:
        cq_raw = proj('cq')
    kr = proj('kr')
    dk = rope64(proj('dk'), 1.0)
    dk_ref[0, rs] = dk.astype(BF16)
    norms['diff_k'] = _chunk_norm2(dk)
    yield
    ckv = _rms(ckv_raw, kvn_ref[...]).astype(BF16)
    dvt = proj('dv').T
    dvt_ref[0, :, rs] = dvt.astype(BF16)
    yield
    k_nope = _dot(ckv, wuk_ref[...])
    if rope:
        kr = _rope(kr, cosm, sinm, firstm, 8)
    mk = jnp.concatenate([k_nope[:, i:i + LANES] + kr for i in range(0, MLA_HEADS * LANES, LANES)], axis=1)
    mk_ref[0, rs] = mk.astype(BF16)
    norms['mla_k'] = _chunk_norm2(mk)
    yield
    mvt = _dot(ckv, wuv_ref[...]).T
    mvt_ref[0, :, rs] = mvt.astype(BF16)
    sk = rope64(proj('sk'), 1.0)
    sk_ref[0, rs] = sk.astype(BF16)
    norms['swa_k'] = _chunk_norm2(sk)
    sv = proj('sv')
    sv_ref[0, rs] = sv.astype(BF16)
    yield

    if with_q:
        mq_ref, dq_ref, sq_ref, g_ref = q_outs
        cq = _rms(cq_raw, qn_ref[...]).astype(BF16)
        dq = rope64(proj('dq'), DIFF_SCALE * LOG2E)
        dq_ref[0, rs] = dq.astype(BF16)
        norms['diff_q'] = _chunk_norm2(dq)
        yield
        q = _dot(cq, wuq_ref[...])
        mla_scale = MLA_SCALE * LOG2E
        mq = []
        for i in range(0, MLA_HEADS * LANES, LANES):
            qi = q[:, i:i + LANES]
            mq.append(_rope(qi, cosm * mla_scale, sinm * mla_scale, firstm, 8) if rope else qi * mla_scale)
        mq = jnp.concatenate(mq, axis=1)
        mq_ref[0, rs] = mq.astype(BF16)
        norms['mla_q'] = _chunk_norm2(mq)
        yield
        sq = rope64(proj('sq'), SWA_SCALE * LOG2E)
        sq_ref[0, rs] = sq.astype(BF16)
        norms['swa_q'] = _chunk_norm2(sq)
        a, b = _COLS['gates']
        for i in range(a, b, 512):
            yield
            g_ref[0, rs, i - a:i - a + 512] = jax.nn.sigmoid(_dot(hb, _w_cols(w_ref, i, i + 512))).astype(BF16)

    vmax = functools.reduce(jnp.maximum, [jnp.max(jnp.max(jnp.abs(v), axis=1, keepdims=True), axis=0, keepdims=True)
                                          for v in (mvt, dvt, sv)])
    bounds = [jnp.max(norms[name], axis=0, keepdims=True) if name in norms else jnp.zeros((1, 1), F32)
              for name in BOUND_ROWS[:6]] + [vmax, jnp.zeros((1, 1), F32)]
    bound_ref[0, rs.start // (rs.stop - rs.start)] = jnp.broadcast_to(jnp.concatenate(bounds, axis=0), (8, LANES))


def _in_proj(x, mods, pre_w, tabs, w_pieces, kvn, wuk, wuv, qn, wuq, *, rope, with_q):
    bsz, n, d = x.shape
    tm = min(ROW_TILE, n)
    sub = min(SUB_ROWS, tm)
    per_batch_mod = mods.shape[0] > 1
    outs = [(MLA_HEADS * LANES, False), (MLA_HEADS * MLA_V, True), (512, False), (512, True),
            (128, False), (128, False)]
    if with_q:
        outs += [(MLA_HEADS * LANES, False), (512, False), (512, False), (3 * D_MODEL, False)]
    row_spec = lambda w: pl.BlockSpec((1, tm, w), lambda b, i: (b, i, 0))
    col_spec = lambda w: pl.BlockSpec((1, w, tm), lambda b, i: (b, 0, i))
    tab_spec = pl.BlockSpec((tm, LANES), lambda b, i: (i, 0))
    kern = functools.partial(_in_proj_kernel, rope=rope, with_q=with_q)
    return pl.pallas_call(
        kern,
        grid=(bsz, n // tm),
        in_specs=[row_spec(d),
                  pl.BlockSpec((1, N_MODS, d), (lambda b, i: (b, 0, 0)) if per_batch_mod else (lambda b, i: (0, 0, 0))),
                  _const_spec((1, d)),
                  tab_spec, tab_spec, tab_spec, tab_spec,
                  *[_const_spec(w.shape) for w in w_pieces], _const_spec(kvn.shape),
                  _const_spec(wuk.shape), _const_spec(wuv.shape), _const_spec(qn.shape), _const_spec(wuq.shape)],
        out_specs=[col_spec(w) if t else row_spec(w) for w, t in outs]
                  + [pl.BlockSpec((1, tm // sub, 8, LANES), lambda b, i: (b, i, 0, 0))],
        out_shape=[jax.ShapeDtypeStruct((bsz, w, n) if t else (bsz, n, w), BF16) for w, t in outs]
                  + [jax.ShapeDtypeStruct((bsz, n // sub, 8, LANES), F32)],
        compiler_params=_params(2),
        name='in_proj_rope' if rope else 'in_proj_ctx',
    )(x, mods, pre_w, *tabs, *w_pieces, kvn, wuk, wuv, qn, wuq)


def _kv_chunks(k_refs, vt_refs, col, vrow, dv):
    chunks = []
    for k_ref, vt_ref in zip(k_refs, vt_refs):
        n = k_ref.shape[1]
        step = min(KEY_CHUNK, n)
        for c in range(0, n, step):
            chunks.append((functools.partial(lambda r, c, s: r[0, c:c + s, col:col + LANES], k_ref, c, step),
                           functools.partial(lambda r, c, s: r[0, vrow:vrow + dv, c:c + s], vt_ref, c, step)))
    return chunks


def _flash_chains(chains):
    n = len(chains[0][1])
    state = [(None, None)] * len(chains)
    s_next = [_dot_nt(chunks[0][0](), q) for q, chunks in chains]
    for i in range(n):
        for ci, (q, chunks) in enumerate(chains):
            m, acc = state[ci]
            s = s_next[ci]
            if i + 1 < n:
                s_next[ci] = _dot_nt(chunks[i + 1][0](), q)
            mc = jnp.max(s, axis=0, keepdims=True)
            m_new = mc if m is None else jnp.maximum(m, mc)
            pv = _dot(_with_ones_rows(chunks[i][1]()), jnp.exp2(s - m_new).astype(BF16))
            acc = pv if m is None else jnp.exp2(m - m_new) * acc + pv
            state[ci] = (m_new, acc)
    return [acc for _, acc in state]


def _unshifted_chains(chains):
    n = len(chains[0][1])
    acc = [None] * len(chains)
    s_next = [_dot_nt(chunks[0][0](), q) for q, chunks in chains]
    for i in range(n):
        for ci, (q, chunks) in enumerate(chains):
            s = s_next[ci]
            if i + 1 < n:
                s_next[ci] = _dot_nt(chunks[i + 1][0](), q)
            pv = _dot(_with_ones_rows(chunks[i][1]()), jnp.exp2(s).astype(BF16))
            acc[ci] = pv if acc[ci] is None else acc[ci] + pv
    return acc


def _with_ones_rows(vt):
    return jnp.concatenate([vt, jnp.ones((BF16_ROWS, vt.shape[1]), BF16)], axis=0)


def _run_chains(chains, bounded, finish):
    @pl.when(bounded)
    def _():
        finish(_unshifted_chains(chains))

    @pl.when(jnp.logical_not(bounded))
    def _():
        finish(_flash_chains(chains))


def _tile_bounded(bounded_ref, tile_axis):
    return bounded_ref[pl.program_id(0) * pl.num_programs(tile_axis) + pl.program_id(tile_axis)] != 0


def _prefetch_grid(grid, in_specs, out_specs):
    return pltpu.PrefetchScalarGridSpec(num_scalar_prefetch=1, grid=grid, in_specs=in_specs, out_specs=out_specs)


def _mla_kernel(bounded_ref, *refs, n_seg):
    q_ref = refs[0]
    k_refs = refs[1:1 + n_seg]
    vt_refs = refs[1 + n_seg:1 + 2 * n_seg]
    o_ref = refs[1 + 2 * n_seg]
    tq = q_ref.shape[1]
    subs = range(0, tq, Q_SUB)
    chains = []
    for j in range(2):
        chunks = _kv_chunks(k_refs, vt_refs, j * LANES, j * MLA_V, MLA_V)
        chains += [(q_ref[0, r:r + Q_SUB, j * LANES:(j + 1) * LANES], chunks) for r in subs]

    def finish(res):
        for ri, r in enumerate(subs):
            pair = [res[j * len(subs) + ri] for j in range(2)]
            o_t = jnp.concatenate([a[:MLA_V] / a[MLA_V:MLA_V + 1] for a in pair], axis=0)
            o_ref[0, r:r + Q_SUB, :] = o_t.T.astype(BF16)

    _run_chains(chains, _tile_bounded(bounded_ref, 2), finish)


def _mla(bounded, q, ks, vts):
    bsz, n, _ = q.shape
    tq = min(Q_TILE, n)
    n_seg = len(ks)
    return pl.pallas_call(
        functools.partial(_mla_kernel, n_seg=n_seg),
        grid_spec=_prefetch_grid(
            (bsz, MLA_HEADS // 2, n // tq),
            [pl.BlockSpec((1, tq, 2 * LANES), lambda b, h, i, _: (b, i, h))]
            + [pl.BlockSpec((1, k.shape[1], 2 * LANES), lambda b, h, i, _: (b, 0, h)) for k in ks]
            + [pl.BlockSpec((1, 2 * MLA_V, v.shape[2]), lambda b, h, i, _: (b, h, 0)) for v in vts],
            pl.BlockSpec((1, tq, LANES), lambda b, h, i, _: (b, i, h))),
        out_shape=jax.ShapeDtypeStruct((bsz, n, MLA_HEADS * MLA_V), BF16),
        compiler_params=_params(3),
        name=f'mla_attn_{n_seg}seg',
    )(bounded, q, *ks, *vts)


def _diff_kernel(bounded_ref, *refs, n_seg, lam_init):
    q_ref, lam_ref, nrm_ref = refs[:3]
    k_refs = refs[3:3 + n_seg]
    vt_refs = refs[3 + n_seg:3 + 2 * n_seg]
    o_ref = refs[3 + 2 * n_seg]
    lv = lam_ref[...]
    lam = (jnp.exp(jnp.sum(lv[0:1] * lv[1:2], axis=-1, keepdims=True))
           - jnp.exp(jnp.sum(lv[2:3] * lv[3:4], axis=-1, keepdims=True)) + lam_init)
    dv = 2 * DIFF_DH
    tq = q_ref.shape[1]
    subs = range(0, tq, Q_SUB)
    lane = lax.broadcasted_iota(jnp.int32, (Q_SUB, LANES), 1)
    zero = jnp.zeros((Q_SUB, LANES), BF16)
    chunks = _kv_chunks(k_refs, vt_refs, 0, 0, dv)
    chains = []
    for comp in range(2):
        keep = (lane < DIFF_DH) if comp == 0 else (lane >= DIFF_DH)
        chains += [(jnp.where(keep, q_ref[0, r:r + Q_SUB, :], zero), chunks) for r in subs]

    def finish(res):
        for ri, r in enumerate(subs):
            a0, a1 = res[ri], res[len(subs) + ri]
            o_t = a0[:dv] / a0[dv:dv + 1] - lam * (a1[:dv] / a1[dv:dv + 1])
            o_ref[0, r:r + Q_SUB, :] = (_rms(o_t.T, nrm_ref[...]) * (1.0 - lam_init)).astype(BF16)

    _run_chains(chains, _tile_bounded(bounded_ref, 2), finish)


def _diff(bounded, q, lam_w, nrm_w, ks, vts, *, lam_init):
    bsz, n, _ = q.shape
    tq = min(Q_TILE, n)
    n_seg = len(ks)
    const = lambda a: pl.BlockSpec(a.shape, lambda b, h, i, _: (0, 0), pipeline_mode=pl.Buffered(1))
    return pl.pallas_call(
        functools.partial(_diff_kernel, n_seg=n_seg, lam_init=lam_init),
        grid_spec=_prefetch_grid(
            (bsz, DIFF_HEADS, n // tq),
            [pl.BlockSpec((1, tq, LANES), lambda b, h, i, _: (b, i, h)), const(lam_w), const(nrm_w)]
            + [pl.BlockSpec((1, k.shape[1], LANES), lambda b, h, i, _: (b, 0, h)) for k in ks]
            + [pl.BlockSpec((1, 2 * DIFF_DH, v.shape[2]), lambda b, h, i, _: (b, h, 0)) for v in vts],
            pl.BlockSpec((1, tq, LANES), lambda b, h, i, _: (b, i, h))),
        out_shape=jax.ShapeDtypeStruct((bsz, n, DIFF_HEADS * 2 * DIFF_DH), BF16),
        compiler_params=_params(3),
        name=f'diff_attn_{n_seg}seg',
    )(bounded, q, lam_w, nrm_w, *ks, *vts)


def _swa_kernel(bounded_ref, *refs, local, seq):
    if local:
        q_ref, sink_ref, kc_ref, vc_ref, kx_ref, vx_ref, o_ref = refs
    else:
        q_ref, sink_ref, kc_ref, vc_ref, o_ref = refs
    blk = WINDOW
    n_blocks = q_ref.shape[1] // blk
    nb = seq // blk
    cols = SWA_GROUP * blk
    g = pl.program_id(1)
    sinks = [sink_ref[half] * LOG2E for half in range(SWA_KV_HEADS)]
    bounded = _tile_bounded(bounded_ref, 1)

    lane = lax.broadcasted_iota(jnp.int32, (blk, LANES), 1)
    kc = kc_ref[0]
    vct = vc_ref[0].T
    if local:
        key_r = lax.broadcasted_iota(jnp.int32, (blk, cols), 0)
        qry_r = lax.broadcasted_iota(jnp.int32, (blk, cols), 1) & (blk - 1)

        def block_rows(ref, idx):
            return ref[0, pl.ds(pl.multiple_of(idx * blk, blk), blk), :]

        first = g * n_blocks
        win = [jnp.clip(first - 1 + t, 0, nb - 1) for t in range(n_blocks + 2)]
        k_blocks = [block_rows(kx_ref, w) for w in win]
        vt_blocks = [block_rows(vx_ref, w).T for w in win]

    def run(shifted):
        pairs = [(b, half) for b in range(n_blocks) for half in range(SWA_KV_HEADS)]
        window = {}
        if local:
            for b in range(n_blocks):
                i = first + b
                window[b] = (jnp.logical_and(qry_r <= key_r, i > 0), jnp.logical_and(key_r <= qry_r, i < nb - 1),
                             jnp.concatenate(k_blocks[b:b + 3], axis=0),
                             jnp.concatenate(vt_blocks[b:b + 3], axis=1))
        scores = {}
        for b, half in pairs:
            keep = (lane < SWA_DH) if half == 0 else (lane >= SWA_DH)
            q = jnp.concatenate(
                [jnp.where(keep, q_ref[0, b * blk:(b + 1) * blk, c * LANES:(c + 1) * LANES],
                           jnp.zeros((blk, LANES), BF16)) for c in range(SWA_GROUP)], axis=0)
            scores[b, half] = (_dot_nt(kc, q), _dot_nt(window[b][2], q) if local else None)
        probs = {}
        for b, half in pairs:
            s_c, s_l = scores[b, half]
            sink = sinks[half]
            if local:
                prev_vis, next_vis = window[b][:2]
            if shifted:
                m = jnp.maximum(jnp.max(s_c, axis=0, keepdims=True), sink)
                if local:
                    neg = -jnp.inf
                    m = jnp.maximum(m, jnp.max(jnp.where(prev_vis, s_l[:blk], neg), axis=0, keepdims=True))
                    m = jnp.maximum(m, jnp.max(s_l[blk:2 * blk], axis=0, keepdims=True))
                    m = jnp.maximum(m, jnp.max(jnp.where(next_vis, s_l[2 * blk:], neg), axis=0, keepdims=True))
                    s_l = s_l - m
                s_c, sink = s_c - m, sink - m
            p_l = None
            if local:
                e_l = jnp.exp2(s_l)
                p_l = jnp.concatenate([jnp.where(prev_vis, e_l[:blk], 0.0), e_l[blk:2 * blk],
                                       jnp.where(next_vis, e_l[2 * blk:], 0.0)], axis=0).astype(BF16)
            probs[b, half] = (jnp.exp2(s_c).astype(BF16), p_l, jnp.exp2(sink))
        outs = {}
        for b, half in pairs:
            p_c, p_l, e_sink = probs[b, half]
            rows = slice(half * SWA_DH, (half + 1) * SWA_DH)
            acc = _dot(_with_ones_rows(vct[rows]), p_c)
            if local:
                acc = acc + _dot(_with_ones_rows(window[b][3][rows]), p_l)
            outs[b, half] = acc[:SWA_DH] / (acc[SWA_DH:SWA_DH + 1] + e_sink)
        for b in range(n_blocks):
            o_t = jnp.concatenate([outs[b, half] for half in range(SWA_KV_HEADS)], axis=0)
            for c in range(SWA_GROUP):
                o_ref[0, b * blk:(b + 1) * blk, c * LANES:(c + 1) * LANES] = \
                    o_t[:, c * blk:(c + 1) * blk].T.astype(BF16)

    @pl.when(bounded)
    def _():
        run(False)

    @pl.when(jnp.logical_not(bounded))
    def _():
        run(True)


def _swa(bounded, q, sink_cols, kc, vc, kx=None, vx=None):
    bsz, n, _ = q.shape
    local = kx is not None
    rows = min(SWA_BLOCKS * WINDOW, n)
    full = lambda a: pl.BlockSpec((1,) + a.shape[1:], lambda b, i, _: (b, 0, 0))
    kv = [kc, vc] + ([kx, vx] if local else [])
    return pl.pallas_call(
        functools.partial(_swa_kernel, local=local, seq=n),
        grid_spec=_prefetch_grid(
            (bsz, n // rows),
            [pl.BlockSpec((1, rows, SWA_HEADS * SWA_DH), lambda b, i, _: (b, i, 0)),
             pl.BlockSpec(sink_cols.shape, lambda b, i, _: (0, 0, 0), pipeline_mode=pl.Buffered(1))]
            + [full(a) for a in kv],
            pl.BlockSpec((1, rows, SWA_HEADS * SWA_DH), lambda b, i, _: (b, i, 0))),
        out_shape=jax.ShapeDtypeStruct((bsz, n, SWA_HEADS * SWA_DH), BF16),
        compiler_params=_params(2),
        name='swa_attn_local' if local else 'swa_attn_ctx',
    )(bounded, q, sink_cols, *kv)


def _merge_kernel(x_ref, mod_ref, post_ref, oa_ref, ob_ref, oc_ref, g_ref,
                  wa_ref, wb_ref, wc_ref, wo_ref, o_ref):
    d = D_MODEL
    y = (g_ref[0, :, 0:d].astype(F32) * _dot(oa_ref[0], wa_ref[...])
         + g_ref[0, :, d:2 * d].astype(F32) * _dot(ob_ref[0], wb_ref[...])
         + g_ref[0, :, 2 * d:3 * d].astype(F32) * _dot(oc_ref[0], wc_ref[...]))
    y = _dot(y.astype(BF16), wo_ref[...])
    mod = mod_ref[0]
    o_ref[0] = x_ref[0] + mod[2:3] * _rms(y, post_ref[...])


def _merge(x, mods, post_w, oa, ob, oc, gates, wa, wb, wc, wo):
    bsz, n, d = x.shape
    tm = min(FF_TILE, n)
    per_batch_mod = mods.shape[0] > 1
    row_spec = lambda w: pl.BlockSpec((1, tm, w), lambda b, i: (b, i, 0))
    return pl.pallas_call(
        _merge_kernel,
        grid=(bsz, n // tm),
        in_specs=[row_spec(d),
                  pl.BlockSpec((1, N_MODS, d), (lambda b, i: (b, 0, 0)) if per_batch_mod else (lambda b, i: (0, 0, 0))),
                  _const_spec((1, d)),
                  row_spec(512), row_spec(512), row_spec(512), row_spec(3 * d),
                  _const_spec(wa.shape), _const_spec(wb.shape), _const_spec(wc.shape), _const_spec(wo.shape)],
        out_specs=row_spec(d),
        out_shape=jax.ShapeDtypeStruct((bsz, n, d), F32),
        compiler_params=_params(2),
        name='merge_out_proj',
    )(x, mods, post_w, oa, ob, oc, gates, wa, wb, wc, wo)


def _ffn_kernel(x_ref, mod_ref, pre_ref, post_ref, wgu_ref, wd_ref, o_ref):
    tm = x_ref.shape[1]
    sub = min(SUB_ROWS, tm)
    mod = mod_ref[0]
    for r0 in range(0, tm, sub):
        rs = slice(r0, r0 + sub)
        x = x_ref[0, rs]
        h = (_rms(x, pre_ref[...]) * (1.0 + mod[4:5]) + mod[3:4]).astype(BF16)
        g = _dot(h, wgu_ref[:, 0:D_FF])
        u = _dot(h, wgu_ref[:, D_FF:2 * D_FF])
        act = (g * jax.nn.sigmoid(g) * u).astype(BF16)
        f = _dot(act, wd_ref[...])
        o_ref[0, rs] = x + mod[5:6] * _rms(f, post_ref[...])


def _ffn(x, mods, pre_w, post_w, wgu, wd):
    bsz, n, d = x.shape
    tm = min(FF_TILE, n)
    per_batch_mod = mods.shape[0] > 1
    row_spec = pl.BlockSpec((1, tm, d), lambda b, i: (b, i, 0))
    return pl.pallas_call(
        _ffn_kernel,
        grid=(bsz, n // tm),
        in_specs=[row_spec,
                  pl.BlockSpec((1, N_MODS, d), (lambda b, i: (b, 0, 0)) if per_batch_mod else (lambda b, i: (0, 0, 0))),
                  _const_spec((1, d)), _const_spec((1, d)),
                  _const_spec(wgu.shape), _const_spec(wd.shape)],
        out_specs=row_spec,
        out_shape=jax.ShapeDtypeStruct((bsz, n, d), F32),
        compiler_params=_params(2),
        name='swiglu',
    )(x, mods, pre_w, post_w, wgu, wd)


def _rope_tables(seq):
    t = np.arange(seq)
    pos = np.stack([t // GRID_W, t % GRID_W]).astype(np.float64)
    lanes = np.arange(LANES)

    def table(rot_dim, slot):
        axis_dim = rot_dim // 2
        half = axis_dim // 2
        inv = ROPE_BASE ** (-np.arange(0, axis_dim, 2, dtype=np.float64) / axis_dim)
        valid = slot >= 0
        axis, j = np.divmod(np.where(valid, slot, 0), axis_dim)
        ang = pos[axis].T * inv[j % half][None, :]
        sign = np.where(j < half, -1.0, 1.0)
        cos = np.where(valid[None, :], np.cos(ang), 1.0)
        sin = np.where(valid[None, :], np.sin(ang) * sign[None, :], 0.0)
        return jnp.asarray(cos, F32), jnp.asarray(sin, F32)

    cos64, sin64 = table(SWA_DH, lanes % SWA_DH)
    in_rope = (lanes >= MLA_NOPE) & (lanes < MLA_NOPE + MLA_ROPE)
    cosm, sinm = table(MLA_ROPE, np.where(in_rope, lanes - MLA_NOPE, -1))
    return cos64, sin64, cosm, sinm


def _pack_in_proj(w):
    d = w.shape[0]
    kr0 = MLA_RANK
    mid0 = kr0 + MLA_ROPE
    sq0 = mid0 + (_COLS['sq'][0] - _COLS['dk'][0])
    gates0 = sq0 + SWA_HEADS * SWA_DH
    kr_wide = jnp.pad(w[:, kr0:mid0], ((0, 0), (MLA_NOPE, LANES - MLA_NOPE - MLA_ROPE)))
    sq = w[:, sq0:gates0].reshape(d, SWA_HEADS, SWA_DH)[:, np.array(SWA_HEAD_PERM)].reshape(d, SWA_HEADS * SWA_DH)
    pieces = (w[:, :kr0], kr_wide, w[:, mid0:sq0], sq, w[:, gates0:])
    wcat = jnp.concatenate(pieces, axis=1)
    assert wcat.shape[1] == IN_COLS
    return (wcat.astype(BF16),)


def _pad_heads(w, used):
    r = w.shape[0]
    w = w.reshape(r, MLA_HEADS, used)
    return jnp.pad(w, ((0, 0), (0, 0), (0, LANES - used))).reshape(r, MLA_HEADS * LANES)


def _bounded_tiles(bound_x, bound_c, sink):
    row = {name: i for i, name in enumerate(BOUND_ROWS)}
    bx, bc = bound_x[:, :, :, 0], bound_c[:, :, :, 0]
    keys = jnp.maximum(jnp.max(bx, axis=1), jnp.max(bc, axis=1))
    values_ok = keys[:, row['v_abs']] <= VALUE_BOUND
    sink_ok = jnp.max(jnp.abs(sink)) * LOG2E <= LOGIT_BOUND
    result = []
    for b, rows_per_sub in ((bx, SUB_ROWS), (bc, min(SUB_ROWS, bc.shape[1] * SUB_ROWS))):
        n = b.shape[1] * rows_per_sub
        per_mixer = {}
        for mixer, tile in (('mla', Q_TILE), ('diff', Q_TILE), ('swa', SWA_BLOCKS * WINDOW)):
            subs_per_tile = min(tile, n) // rows_per_sub
            q2 = jnp.max(b[:, :, row[mixer + '_q']].reshape(b.shape[0], -1, subs_per_tile), axis=2)
            ok = q2 * keys[:, None, row[mixer + '_k']] * BF16_NORM2_SLACK <= LOGIT_BOUND * LOGIT_BOUND
            ok = ok & values_ok[:, None] & (sink_ok if mixer == 'swa' else True)
            per_mixer[mixer] = ok.astype(jnp.int32).reshape(-1)
        result.append(per_mixer)
    return result


def kernel(x, c, ctx, c_ctx, w_ada, b_ada, attn_pre_norm, attn_post_norm, ffn_pre_norm, ffn_post_norm, w_in, mla_q_norm, w_uq, mla_kv_norm, w_ukv, diff_lambda, diff_norm, swa_sink, w_branch_a, w_branch_b, w_branch_c, w_o, w_gate_up, w_down):
    bsz, seq, d = x.shape
    depth = w_ada.shape[0]
    tabs = _rope_tables(seq)
    cond = jnp.concatenate([c, c_ctx[None, :], jnp.zeros((8 - bsz - 1, d), c.dtype)], axis=0)
    row = lambda v: v.reshape(1, -1)
    all_mods = _adaln(cond, w_ada, b_ada)

    for l in range(depth):
        last = l == depth - 1
        lam_init = 0.8 - 0.6 * math.exp(-0.3 * l)
        mods = all_mods[l]
        mx = mods[:bsz].reshape(bsz, N_MODS, d)
        mc = mods[bsz:bsz + 1].reshape(1, N_MODS, d)

        w_pieces = _pack_in_proj(w_in[l])
        ukv = w_ukv[l].reshape(MLA_RANK, MLA_HEADS, MLA_NOPE + MLA_V)
        wuk = _pad_heads(ukv[:, :, :MLA_NOPE].reshape(MLA_RANK, -1), MLA_NOPE).astype(BF16)
        wuv = ukv[:, :, MLA_NOPE:].reshape(MLA_RANK, MLA_HEADS * MLA_V).astype(BF16)
        wuq = _pad_heads(w_uq[l], MLA_NOPE + MLA_ROPE).astype(BF16)
        proj_w = (w_pieces, row(mla_kv_norm[l]), wuk, wuv, row(mla_q_norm[l]), wuq)
        sink_cols = jnp.repeat(swa_sink[l].reshape(SWA_KV_HEADS, 1, SWA_GROUP), WINDOW, axis=2)
        wa, wb = w_branch_a[l].astype(BF16), w_branch_b[l].astype(BF16)
        wc = w_branch_c[l].reshape(SWA_HEADS, SWA_DH, d)[np.array(SWA_HEAD_PERM)].reshape(-1, d).astype(BF16)
        wo, wgu, wd = w_o[l].astype(BF16), w_gate_up[l].astype(BF16), w_down[l].astype(BF16)
        dnorm = row(diff_norm[l])
        pre_a, post_a = row(attn_pre_norm[l]), row(attn_post_norm[l])
        pre_f, post_f = row(ffn_pre_norm[l]), row(ffn_post_norm[l])

        px = _in_proj(x, mx, pre_a, tabs, *proj_w, rope=True, with_q=True)
        pc = _in_proj(ctx, mc, pre_a, tabs, *proj_w, rope=False, with_q=not last)
        mkx, mvx, dkx, dvx, skx, svx, mqx, dqx, sqx, gx, bound_x = px
        mkc, mvc, dkc, dvc, skc, svc = pc[:6]
        bounded_x, bounded_c = _bounded_tiles(bound_x, pc[-1], swa_sink[l])
        o_a = _mla(bounded_x['mla'], mqx, [mkc, mkx], [mvc, mvx])
        o_b = _diff(bounded_x['diff'], dqx, diff_lambda[l], dnorm, [dkc, dkx], [dvc, dvx], lam_init=lam_init)
        o_c = _swa(bounded_x['swa'], sqx, sink_cols, skc, svc, skx, svx)
        x = _merge(x, mx, post_a, o_a, o_b, o_c, gx, wa, wb, wc, wo)
        if not last:
            mqc, dqc, sqc, gc = pc[6:-1]
            p_a = _mla(bounded_c['mla'], mqc, [mkc], [mvc])
            p_b = _diff(bounded_c['diff'], dqc, diff_lambda[l], dnorm, [dkc], [dvc], lam_init=lam_init)
            p_c = _swa(bounded_c['swa'], sqc, sink_cols, skc, svc)
            ctx = _merge(ctx, mc, post_a, p_a, p_b, p_c, gc, wa, wb, wc, wo)

        x = _ffn(x, mx, pre_f, post_f, wgu, wd)
        if not last:
            ctx = _ffn(ctx, mc, pre_f, post_f, wgu, wd)
    return x
```
